```python
import math
import jax
import jax.numpy as jnp
from jax import lax
import numpy as np

D_MODEL = 1024
BATCH = 8
SEQ = 4096
DEPTH = 1

GRID_W = 64
CTX_LEN = 256
EPS = 1e-6

S5_WIDTH = D_MODEL
S5_GROUP = 16
S5_GROUPS = S5_WIDTH // S5_GROUP
S5_STATE = 64
S5_DT_MIN = 1e-3
S5_DT_MAX = 1e-1

SSD_WIDTH = 2 * D_MODEL
SSD_HEADDIM = 64
SSD_HEADS = SSD_WIDTH // SSD_HEADDIM
SSD_GROUPS = 8
SSD_HPG = SSD_HEADS // SSD_GROUPS
SSD_STATE = 128
SSD_CONV = 5
SSD_CHUNK = 128
SSD_CONV_DIM = SSD_WIDTH + 2 * SSD_GROUPS * SSD_STATE
SSD_DT_MIN = 1e-3
SSD_DT_MAX = 1e-1

IN_STATE_COLS = S5_WIDTH + SSD_CONV_DIM + 2 * SSD_HEADS
IN_COLS = IN_STATE_COLS + SSD_WIDTH + 2 * D_MODEL
IN_SPLITS = (S5_WIDTH, S5_WIDTH + SSD_CONV_DIM, IN_STATE_COLS, IN_STATE_COLS + SSD_WIDTH, IN_STATE_COLS + SSD_WIDTH + D_MODEL)

N_EXPERTS = 32
TOP_K = 4
D_EXPERT = D_MODEL
SWIGLU_ALPHA = 1.702
SWIGLU_LIMIT = 7.0
MOE_BLOCK = 128

kernel_name = 'hybrid_s5_ssd_moe_dit_block'


def rms_norm(h, w):
    hf = h.astype(jnp.float32)
    hf = hf * lax.rsqrt(jnp.mean(hf * hf, axis=-1, keepdims=True) + EPS)
    return hf.astype(h.dtype) * w


def ada_modulation(cond, w, b):
    m = jax.nn.silu(cond) @ w + b
    return jnp.split(m[:, None, :], 6, axis=-1)


def to_col_major(h):
    b_, l, ch = h.shape
    rows = l // GRID_W
    return h.reshape(b_, rows, GRID_W, ch).transpose(0, 2, 1, 3).reshape(b_, l, ch)


def to_row_major(h):
    b_, l, ch = h.shape
    rows = l // GRID_W
    return h.reshape(b_, GRID_W, rows, ch).transpose(0, 2, 1, 3).reshape(b_, l, ch)


def s5_discretise(lam_re, lam_im, log_dt, b_re, b_im):
    f32 = jnp.float32
    lam = lax.complex(lam_re.astype(f32), lam_im.astype(f32))
    step = jnp.exp(log_dt.astype(f32))[:, None]
    a_bar = jnp.exp(lam * step)
    b_mat = lax.complex(b_re.astype(f32), b_im.astype(f32))
    b_bar = ((a_bar - 1.0) / lam)[:, :, None] * b_mat
    return a_bar, b_bar


def _diag_linear_combine(left, right):
    a_l, b_l = left
    a_r, b_r = right
    return a_l * a_r, a_r * b_l + b_r


def s5_states(u, a_bar, b_bar, init, reverse):
    bu = jnp.einsum('blgh,gph->lbgp', u.astype(jnp.complex64), b_bar)
    if init is not None:
        edge = -1 if reverse else 0
        bu = bu.at[edge].add(a_bar * init)
    a = jnp.broadcast_to(a_bar, (bu.shape[0], 1) + a_bar.shape)
    _, states = lax.associative_scan(_diag_linear_combine, (a, bu), reverse=reverse, axis=0)
    return states


def s5_readout(u, states_f, states_b, p):
    f32 = jnp.float32
    c_f = lax.complex(p['s5_c_re'][0].astype(f32), p['s5_c_im'][0].astype(f32))
    c_b = lax.complex(p['s5_c_re'][1].astype(f32), p['s5_c_im'][1].astype(f32))
    y = (jnp.einsum('lbgp,ghp->blgh', states_f, c_f).real
         + jnp.einsum('lbgp,ghp->blgh', states_b, c_b).real)
    b_, l = u.shape[:2]
    y = y.reshape(b_, l, S5_WIDTH) + p['s5_d'].astype(f32) * u.reshape(b_, l, S5_WIDTH)
    g = jax.nn.gelu(y)
    return g * jax.nn.sigmoid(g @ p['s5_glu_w'].astype(f32) + p['s5_glu_b'].astype(f32))


def s5_mixer(u_ctx, u_lat, p, with_ctx_out):
    dtype = u_lat.dtype
    b_, lc, _ = u_ctx.shape
    l = u_lat.shape[1]
    uc = u_ctx.astype(jnp.float32).reshape(b_, lc, S5_GROUPS, S5_GROUP)
    ul = u_lat.astype(jnp.float32).reshape(b_, l, S5_GROUPS, S5_GROUP)
    a_f, b_f = s5_discretise(p['s5_lam_re'][0], p['s5_lam_im'][0], p['s5_log_dt'][0], p['s5_b_re'][0], p['s5_b_im'][0])
    a_b, b_b = s5_discretise(p['s5_lam_re'][1], p['s5_lam_im'][1], p['s5_log_dt'][1], p['s5_b_re'][1], p['s5_b_im'][1])
    ctx_f = s5_states(uc, a_f, b_f, None, False)
    ctx_b = s5_states(uc, a_b, b_b, None, True)
    lat_f = s5_states(ul, a_f, b_f, ctx_f[-1], False)
    lat_b = s5_states(ul, a_b, b_b, ctx_b[0], True)
    y_lat = s5_readout(ul, lat_f, lat_b, p).astype(dtype)
    y_ctx = s5_readout(uc, ctx_f, ctx_b, p).astype(dtype) if with_ctx_out else None
    return y_lat, y_ctx


def depthwise_conv_centred(h, w, b):
    pad = w.shape[0] // 2
    out = lax.conv_general_dilated(h, w[:, None, :], window_strides=(1,), padding=((pad, pad),),
                                   dimension_numbers=('NWC', 'WIO', 'NWC'), feature_group_count=h.shape[-1])
    return out + b


def ssd_prep(xbc, dt_raw, p):
    f32 = jnp.float32
    xbc = jax.nn.silu(depthwise_conv_centred(xbc, p['ssd_conv_w'], p['ssd_conv_b'])).astype(f32)
    b_, l, _ = xbc.shape
    xs, bm, cm = jnp.split(xbc, [SSD_WIDTH, SSD_WIDTH + SSD_GROUPS * SSD_STATE], axis=-1)
    xs = xs.reshape(b_, l, SSD_HEADS, SSD_HEADDIM)
    bm = bm.reshape(b_, l, SSD_GROUPS, SSD_STATE)
    cm = cm.reshape(b_, l, SSD_GROUPS, SSD_STATE)
    dt = jax.nn.softplus(dt_raw.astype(f32).reshape(b_, l, 2, SSD_HEADS) + p['ssd_dt_bias'].astype(f32))
    return xs, bm, cm, dt[:, :, 0], dt[:, :, 1]


def decay_matrix(a_cum):
    q = a_cum.shape[-1]
    mask = jnp.tril(jnp.ones((q, q), dtype=bool))
    diff = a_cum[..., :, None] - a_cum[..., None, :]
    return jnp.where(mask, jnp.exp(jnp.where(mask, diff, 0.0)), 0.0)


def ssd_scan(xs, dt, a, bm, cm, init, with_output):
    b_, l = xs.shape[:2]
    nc = l // SSD_CHUNK
    shp = (b_, nc, SSD_CHUNK, SSD_GROUPS)
    x = xs.reshape(shp + (SSD_HPG, SSD_HEADDIM))
    dtc = dt.reshape(shp + (SSD_HPG,))
    bc = bm.reshape(shp + (SSD_STATE,))
    cc = cm.reshape(shp + (SSD_STATE,))
    a_cum = jnp.moveaxis(jnp.cumsum(dtc * a.reshape(SSD_GROUPS, SSD_HPG), axis=2), 2, -1)
    dt_t = jnp.moveaxis(dtc, 2, -1)
    w_end = jnp.exp(a_cum[..., -1:] - a_cum) * dt_t
    chunk_states = jnp.einsum('bcsgn,bcgrs,bcsgrp->bcgrpn', bc, w_end, x)
    chunk_decay = jnp.exp(a_cum[..., -1])

    def carry_step(state, inp):
        decay, new = inp
        return decay[..., None, None] * state + new, state

    final, prev = lax.scan(carry_step, init, (jnp.moveaxis(chunk_decay, 1, 0), jnp.moveaxis(chunk_states, 1, 0)))
    if not with_output:
        return None, final
    prev = jnp.moveaxis(prev, 0, 1)
    cb = jnp.einsum('bclgn,bcsgn->bcgls', cc, bc)
    lmat = decay_matrix(a_cum) * dt_t[..., None, :]
    y_diag = jnp.einsum('bcgls,bcgrls,bcsgrp->bclgrp', cb, lmat, x)
    y_off = jnp.einsum('bclgn,bcgrpn,bcgrl->bclgrp', cc, prev, jnp.exp(a_cum))
    return (y_diag + y_off).reshape(b_, l, SSD_HEADS, SSD_HEADDIM), final


def ssd_bidirectional(xs, bm, cm, dt_f, dt_b, a, init_f, init_b, with_output):
    y_f, fin_f = ssd_scan(xs, dt_f, a[0], bm, cm, init_f, with_output)
    flip = lambda t: jnp.flip(t, axis=1)
    y_b, fin_b = ssd_scan(flip(xs), flip(dt_b), a[1], flip(bm), flip(cm), init_b, with_output)
    y = y_f + flip(y_b) if with_output else None
    return y, fin_f, fin_b


def ssd_output(y, xs, z, p):
    f32 = jnp.float32
    b_, l = xs.shape[:2]
    y = (y + p['ssd_d'].astype(f32)[:, None] * xs).reshape(b_, l, SSD_WIDTH)
    y = y * jax.nn.silu(z.astype(f32))
    yg = y.reshape(b_, l, SSD_GROUPS, SSD_WIDTH // SSD_GROUPS)
    yg = yg * lax.rsqrt(jnp.mean(yg * yg, axis=-1, keepdims=True) + EPS)
    return yg.reshape(b_, l, SSD_WIDTH).astype(z.dtype) * p['ssd_norm_w']


def ssd_mixer(xbc_ctx, dt_ctx, z_ctx, xbc_lat, dt_lat, z_lat, p, with_ctx_out):
    a = -jnp.exp(p['ssd_a_log'].astype(jnp.float32))
    b_ = xbc_lat.shape[0]
    zero_state = jnp.zeros((b_, SSD_GROUPS, SSD_HPG, SSD_HEADDIM, SSD_STATE), jnp.float32)
    xc, bc, cc, dfc, dbc = ssd_prep(xbc_ctx, dt_ctx, p)
    y_ctx, s_f, s_b = ssd_bidirectional(xc, bc, cc, dfc, dbc, a, zero_state, zero_state, with_ctx_out)
    xl, bl, cl, dfl, dbl = ssd_prep(to_col_major(xbc_lat), to_col_major(dt_lat), p)
    y_lat, _, _ = ssd_bidirectional(xl, bl, cl, dfl, dbl, a, s_f, s_b, True)
    y_lat = to_row_major(ssd_output(y_lat, xl, to_col_major(z_lat), p))
    y_ctx = ssd_output(y_ctx, xc, z_ctx, p) if with_ctx_out else None
    return y_lat, y_ctx


def moe_ffn(h, p):
    b_, l, d = h.shape
    xf = h.reshape(b_ * l, d)
    n = xf.shape[0]
    logits = (xf @ p['router_w'] + p['router_b']).astype(jnp.float32)
    top_val, top_idx = lax.top_k(logits, TOP_K)
    gates = jax.nn.softmax(top_val, axis=-1)
    n_pairs = n * TOP_K
    n_blocks = -(-n_pairs // MOE_BLOCK) + N_EXPERTS
    pair_expert = top_idx.reshape(n_pairs)
    pair_token = jnp.arange(n_pairs, dtype=jnp.int32) // TOP_K
    order = jnp.argsort(pair_expert)
    sorted_expert = pair_expert[order]
    counts = jnp.bincount(pair_expert, length=N_EXPERTS)
    padded = (counts + MOE_BLOCK - 1) // MOE_BLOCK * MOE_BLOCK
    padded_end = jnp.cumsum(padded)
    rank = jnp.arange(n_pairs, dtype=jnp.int32) - (jnp.cumsum(counts) - counts)[sorted_expert]
    dest = (padded_end - padded)[sorted_expert] + rank
    n_slots = n_blocks * MOE_BLOCK
    slot_token = jnp.full((n_slots,), n, jnp.int32).at[dest].set(pair_token[order])
    slot_gate = jnp.zeros((n_slots,), jnp.float32).at[dest].set(gates.reshape(n_pairs)[order])
    block_start = jnp.arange(n_blocks, dtype=padded_end.dtype) * MOE_BLOCK
    block_expert = jnp.minimum(jnp.searchsorted(padded_end, block_start, side='right'), N_EXPERTS - 1)
    x_pad = jnp.concatenate([xf, jnp.zeros((1, d), xf.dtype)], axis=0)

    def expert_block(args):
        tokens, e = args
        xb = x_pad[tokens]
        gate = jnp.minimum(xb @ p['moe_w_gate'][e] + p['moe_b_gate'][e], SWIGLU_LIMIT)
        up = jnp.clip(xb @ p['moe_w_up'][e] + p['moe_b_up'][e], -SWIGLU_LIMIT, SWIGLU_LIMIT)
        act = (up + 1.0) * gate * jax.nn.sigmoid(SWIGLU_ALPHA * gate)
        return act @ p['moe_w_down'][e] + p['moe_b_down'][e]

    y = lax.map(expert_block, (slot_token.reshape(n_blocks, MOE_BLOCK), block_expert))
    y = y.reshape(n_slots, d) * slot_gate[:, None].astype(y.dtype)
    return jax.ops.segment_sum(y, slot_token, num_segments=n + 1)[:n].reshape(b_, l, d)


def hybrid_layer(h_lat, h_ctx, c, c_ctx, p, with_ctx_out):
    sh_m, sc_m, g_m, sh_f, sc_f, g_f = ada_modulation(c, p['ada_w'], p['ada_b'])
    csh_m, csc_m, cg_m, csh_f, csc_f, cg_f = ada_modulation(c_ctx[None, :], p['ada_w'], p['ada_b'])
    u_lat = rms_norm(h_lat, p['norm_mix_w']) * (1.0 + sc_m) + sh_m
    u_ctx = rms_norm(h_ctx, p['norm_mix_w']) * (1.0 + csc_m) + csh_m
    proj_lat = u_lat @ p['w_in']
    s5_l, xbc_l, dt_l, z_l, ga_l, gb_l = jnp.split(proj_lat, IN_SPLITS, axis=-1)
    if with_ctx_out:
        proj_ctx = u_ctx @ p['w_in']
        s5_c, xbc_c, dt_c, z_c, ga_c, gb_c = jnp.split(proj_ctx, IN_SPLITS, axis=-1)
    else:
        proj_ctx = u_ctx @ p['w_in'][:, :IN_STATE_COLS]
        s5_c, xbc_c, dt_c = jnp.split(proj_ctx, IN_SPLITS[:2], axis=-1)
        z_c = None
    ya_l, ya_c = s5_mixer(s5_c, s5_l, p, with_ctx_out)
    yb_l, yb_c = ssd_mixer(xbc_c, dt_c, z_c, xbc_l, dt_l, z_l, p, with_ctx_out)

    def merge(ya, yb, ga, gb):
        branch_a = ya @ p['w_branch_a']
        branch_b = yb @ p['w_branch_b']
        return (jax.nn.sigmoid(ga) * branch_a + jax.nn.sigmoid(gb) * branch_b) @ p['w_out']

    def ffn(h, shift, scale):
        return moe_ffn(rms_norm(h, p['norm_ffn_w']) * (1.0 + scale) + shift, p)

    h_lat = h_lat + g_m * merge(ya_l, yb_l, ga_l, gb_l)
    h_lat = h_lat + g_f * ffn(h_lat, sh_f, sc_f)
    if with_ctx_out:
        h_ctx = h_ctx + cg_m * merge(ya_c, yb_c, ga_c, gb_c)
        h_ctx = h_ctx + cg_f * ffn(h_ctx, csh_f, csc_f)
    return h_lat, h_ctx


def setup_inputs(seed: int = 0) -> dict:
    key = jax.random.key(seed)
    ks = iter(jax.random.split(key, 40))
    f32 = jnp.float32
    L, D = DEPTH, D_MODEL

    def nrm(shape, scale=1.0):
        return scale * jax.random.normal(next(ks), shape, f32)

    def unif(shape, lo, hi):
        return jax.random.uniform(next(ks), shape, f32, lo, hi)

    inp = {}
    inp['x'] = nrm((BATCH, SEQ, D))
    inp['c'] = nrm((BATCH, D))
    inp['ctx'] = nrm((BATCH, CTX_LEN, D))
    inp['c_ctx'] = nrm((D,))
    inp['ada_w'] = nrm((L, D, 6 * D), 0.5 * D ** -0.5)
    inp['ada_b'] = nrm((L, 6 * D), 0.01)
    inp['norm_mix_w'] = 1.0 + nrm((L, D), 0.02)
    inp['w_in'] = nrm((L, D, IN_COLS), D ** -0.5)
    n_idx = jnp.arange(S5_STATE, dtype=f32)
    inp['s5_lam_re'] = -0.5 + nrm((L, 2, S5_GROUPS, S5_STATE), 0.01)
    inp['s5_lam_im'] = math.pi * n_idx + nrm((L, 2, S5_GROUPS, S5_STATE), 0.01)
    inp['s5_log_dt'] = unif((L, 2, S5_GROUPS), math.log(S5_DT_MIN), math.log(S5_DT_MAX))
    inp['s5_b_re'] = nrm((L, 2, S5_GROUPS, S5_STATE, S5_GROUP), (2 * S5_GROUP) ** -0.5)
    inp['s5_b_im'] = nrm((L, 2, S5_GROUPS, S5_STATE, S5_GROUP), (2 * S5_GROUP) ** -0.5)
    inp['s5_c_re'] = nrm((L, 2, S5_GROUPS, S5_GROUP, S5_STATE), (2 * S5_STATE) ** -0.5)
    inp['s5_c_im'] = nrm((L, 2, S5_GROUPS, S5_GROUP, S5_STATE), (2 * S5_STATE) ** -0.5)
    inp['s5_d'] = nrm((L, S5_WIDTH))
    inp['s5_glu_w'] = nrm((L, S5_WIDTH, S5_WIDTH), S5_WIDTH ** -0.5)
    inp['s5_glu_b'] = nrm((L, S5_WIDTH), 0.01)
    inp['ssd_conv_w'] = nrm((L, SSD_CONV, SSD_CONV_DIM), SSD_CONV ** -0.5)
    inp['ssd_conv_b'] = nrm((L, SSD_CONV_DIM), 0.01)
    dt0 = jnp.exp(unif((L, 2, SSD_HEADS), math.log(SSD_DT_MIN), math.log(SSD_DT_MAX)))
    inp['ssd_dt_bias'] = dt0 + jnp.log(-jnp.expm1(-dt0))
    inp['ssd_a_log'] = jnp.log(unif((L, 2, SSD_HEADS), 1.0, 16.0))
    inp['ssd_d'] = 1.0 + nrm((L, SSD_HEADS), 0.1)
    inp['ssd_norm_w'] = 1.0 + nrm((L, SSD_WIDTH), 0.02)
    inp['w_branch_a'] = nrm((L, S5_WIDTH, D), S5_WIDTH ** -0.5)
    inp['w_branch_b'] = nrm((L, SSD_WIDTH, D), SSD_WIDTH ** -0.5)
    inp['w_out'] = nrm((L, D, D), D ** -0.5)
    inp['norm_ffn_w'] = 1.0 + nrm((L, D), 0.02)
    inp['router_w'] = nrm((L, D, N_EXPERTS), D ** -0.5)
    inp['router_b'] = nrm((L, N_EXPERTS), 0.01)
    inp['moe_w_gate'] = nrm((L, N_EXPERTS, D, D_EXPERT), D ** -0.5)
    inp['moe_b_gate'] = nrm((L, N_EXPERTS, D_EXPERT), 0.01)
    inp['moe_w_up'] = nrm((L, N_EXPERTS, D, D_EXPERT), D ** -0.5)
    inp['moe_b_up'] = nrm((L, N_EXPERTS, D_EXPERT), 0.01)
    inp['moe_w_down'] = nrm((L, N_EXPERTS, D_EXPERT, D), D_EXPERT ** -0.5)
    inp['moe_b_down'] = nrm((L, N_EXPERTS, D), 0.01)
    inp['final_norm_w'] = 1.0 + nrm((D,), 0.02)
    return inp


def reference(x, c, ctx, c_ctx, ada_w, ada_b, norm_mix_w, w_in, s5_lam_re, s5_lam_im, s5_log_dt,
              s5_b_re, s5_b_im, s5_c_re, s5_c_im, s5_d, s5_glu_w, s5_glu_b, ssd_conv_w, ssd_conv_b,
              ssd_dt_bias, ssd_a_log, ssd_d, ssd_norm_w, w_branch_a, w_branch_b, w_out, norm_ffn_w,
              router_w, router_b, moe_w_gate, moe_b_gate, moe_w_up, moe_b_up, moe_w_down, moe_b_down,
              final_norm_w):
    h_lat, h_ctx = x, ctx
    for layer in range(DEPTH):
        p = dict(
            ada_w=ada_w[layer], ada_b=ada_b[layer], norm_mix_w=norm_mix_w[layer], w_in=w_in[layer],
            s5_lam_re=s5_lam_re[layer], s5_lam_im=s5_lam_im[layer], s5_log_dt=s5_log_dt[layer],
            s5_b_re=s5_b_re[layer], s5_b_im=s5_b_im[layer], s5_c_re=s5_c_re[layer], s5_c_im=s5_c_im[layer],
            s5_d=s5_d[layer], s5_glu_w=s5_glu_w[layer], s5_glu_b=s5_glu_b[layer],
            ssd_conv_w=ssd_conv_w[layer], ssd_conv_b=ssd_conv_b[layer], ssd_dt_bias=ssd_dt_bias[layer],
            ssd_a_log=ssd_a_log[layer], ssd_d=ssd_d[layer], ssd_norm_w=ssd_norm_w[layer],
            w_branch_a=w_branch_a[layer], w_branch_b=w_branch_b[layer], w_out=w_out[layer],
            norm_ffn_w=norm_ffn_w[layer], router_w=router_w[layer], router_b=router_b[layer],
            moe_w_gate=moe_w_gate[layer], moe_b_gate=moe_b_gate[layer], moe_w_up=moe_w_up[layer],
            moe_b_up=moe_b_up[layer], moe_w_down=moe_w_down[layer], moe_b_down=moe_b_down[layer],
        )
        h_lat, h_ctx = hybrid_layer(h_lat, h_ctx, c, c_ctx, p, layer < DEPTH - 1)
    return rms_norm(h_lat, final_norm_w)
```

```python
import functools
import math

import jax
import jax.numpy as jnp
from jax import lax
from jax.experimental import pallas as pl
from jax.experimental.pallas import tpu as pltpu

F32 = jnp.float32
BF16 = jnp.bfloat16

EPS = 1e-6
GRID_W = 64
LANES = 128

S5_GROUP = 16
S5_STATE = 64
S5_COLS = 256
S5_CHUNK = 128

SSD_HEADDIM = 64
SSD_GROUPS = 8
SSD_STATE = 128
SSD_CHUNK = 128
DT_PAD = 256

N_EXPERTS = 32
TOP_K = 4
SWIGLU_ALPHA = 1.702
SWIGLU_LIMIT = 7.0
MOE_BLOCK = 256
ROUTER_PAD = 128

VMEM_LIMIT_BYTES = 56 * 1024 * 1024


def _cparams(sem):
    return pltpu.CompilerParams(dimension_semantics=sem, vmem_limit_bytes=VMEM_LIMIT_BYTES)


def _dot(a, b):
    return jnp.dot(a, b, preferred_element_type=F32)


def _split3(x):
    hi = x.astype(BF16)
    r = x - hi.astype(F32)
    mid = r.astype(BF16)
    lo = (r - mid.astype(F32)).astype(BF16)
    return hi, mid, lo


def _sigmoid(x):
    return 1.0 / (1.0 + jnp.exp(-x))


def _silu(x):
    return x * _sigmoid(x)


def _softplus(x):
    return jnp.maximum(x, 0.0) + jnp.log1p(jnp.exp(-jnp.abs(x)))


def _gelu_tanh(x):
    c = math.sqrt(2.0 / math.pi)
    return 0.5 * x * (1.0 + jnp.tanh(c * (x + 0.044715 * (x * x * x))))


def _rms(x):
    return x * lax.rsqrt(jnp.mean(x * x, axis=-1, keepdims=True) + EPS)


def _ada_kernel(c_ref, w_ref, b_ref, o_ref):
    s = _silu(c_ref[...]).astype(BF16)
    o_ref[...] = _dot(s, w_ref[...].astype(BF16)) + b_ref[...]


def ada_modulation(cond, w, b):
    n, d = cond.shape
    cols = w.shape[1]
    tn = 1024
    return pl.pallas_call(
        _ada_kernel,
        grid=(cols // tn,),
        in_specs=[pl.BlockSpec((n, d), lambda j: (0, 0)),
                  pl.BlockSpec((d, tn), lambda j: (0, j)),
                  pl.BlockSpec((1, tn), lambda j: (0, j))],
        out_specs=pl.BlockSpec((n, tn), lambda j: (0, j)),
        out_shape=jax.ShapeDtypeStruct((n, cols), F32),
        compiler_params=_cparams(("arbitrary",)),
        name="ada_modulation",
    )(cond, w, b.reshape(1, cols))


def _inproj_kernel(x_ref, nw_ref, sc_ref, sh_ref, w_ref, *rest, d_model, widths):
    o_refs = rest[:len(widths)]
    u_ref = rest[len(widths)]
    r = x_ref.shape[1]
    k = x_ref.shape[2] // d_model
    nw = nw_ref[...]
    sc = 1.0 + sc_ref[0]
    sh = sh_ref[0]
    for c in range(k):
        xs = x_ref[0, :, c * d_model:(c + 1) * d_model]
        u_ref[c * r:(c + 1) * r, :] = ((_rms(xs) * nw) * sc + sh).astype(BF16)
    off = 0
    for o_ref, wd in zip(o_refs, widths):
        step = min(wd, 512)
        for n0 in range(0, wd, step):
            o_ref[0, :, n0:n0 + step] = _dot(u_ref[...], w_ref[:, off + n0:off + n0 + step]).astype(o_ref.dtype)
        off += wd


def in_projection(x, norm_w, scale, shift, w, widths, *, col_major, tokens_per_step):
    b_, l, d = x.shape
    if col_major:
        rows = l // GRID_W
        k = tokens_per_step // rows
        xv = x.reshape(b_, rows, GRID_W * d)
        x_spec = pl.BlockSpec((1, rows, k * d), lambda b, i: (b, 0, i))
        steps = GRID_W // k
    else:
        xv = x
        x_spec = pl.BlockSpec((1, tokens_per_step, d), lambda b, i: (b, i, 0))
        steps = l // tokens_per_step
    vec = lambda: pl.BlockSpec((1, 1, d), lambda b, i: (b, 0, 0))
    return pl.pallas_call(
        functools.partial(_inproj_kernel, d_model=d, widths=tuple(widths)),
        grid=(b_, steps),
        in_specs=[x_spec,
                  pl.BlockSpec((1, d), lambda b, i: (0, 0)),
                  vec(), vec(),
                  pl.BlockSpec(w.shape, lambda b, i: (0, 0), pipeline_mode=pl.Buffered(1))],
        out_specs=[pl.BlockSpec((1, tokens_per_step, wd), lambda b, i: (b, i, 0)) for wd in widths],
        out_shape=[jax.ShapeDtypeStruct((b_, l, wd), F32) for wd in widths],
        scratch_shapes=[pltpu.VMEM((tokens_per_step, d), BF16)],
        compiler_params=_cparams(("parallel", "arbitrary")),
        name="in_projection_col" if col_major else "in_projection_row",
    )(xv, norm_w.reshape(1, d), scale, shift, w)


def _s5_kernel(uc_ref, ul_ref, bexp_ref, cexp_ref, ar_ref, ai_ref, y_ref, *scratch, n_ctx):
    nslab = (len(scratch) - 2) // 2
    ubufs, ybufs = scratch[:nslab], scratch[nslab:2 * nslab]
    bu, state = scratch[2 * nslab:]
    d = pl.program_id(0)
    i = pl.program_id(2)
    nb, tl = ul_ref.shape[0], ul_ref.shape[1]
    half = state.shape[1] // 2
    is_ctx = i < n_ctx

    @pl.when(i == 0)
    def _():
        state[...] = jnp.zeros_like(state)

    for b in range(nb):
        ub = jnp.where(is_ctx, uc_ref[b], ul_ref[b])
        for s, ubuf in enumerate(ubufs):
            ubuf[pl.ds(b, tl, stride=nb), :] = ub[:, s * LANES:(s + 1) * LANES]
    u_tm = jnp.concatenate([ubuf[...] for ubuf in ubufs], axis=1)
    bu[...] = _dot(u_tm.astype(BF16), bexp_ref[0, 0])

    ar = ar_ref[0, 0]
    ai = ai_ref[0, 0]

    def step(t, carry):
        xr, xi = carry
        tok = jnp.where(d == 0, t, tl - 1 - t)
        row = pl.multiple_of(tok * nb, nb)
        br = bu[pl.ds(row, nb), 0:half]
        bi = bu[pl.ds(row, nb), half:2 * half]
        nxr = ar * xr - ai * xi + br
        nxi = ar * xi + ai * xr + bi
        bu[pl.ds(row, nb), 0:half] = nxr
        bu[pl.ds(row, nb), half:2 * half] = nxi
        return nxr, nxi

    xr, xi = lax.fori_loop(0, tl, step, (state[:, 0:half], state[:, half:2 * half]), unroll=2)
    state[:, 0:half] = xr
    state[:, half:2 * half] = xi

    @pl.when(jnp.logical_not(is_ctx))
    def _():
        y_tm = _dot(bu[...].astype(BF16), cexp_ref[0, 0])
        for s, ybuf in enumerate(ybufs):
            ybuf[...] = y_tm[:, s * LANES:(s + 1) * LANES]
        for b in range(nb):
            for s, ybuf in enumerate(ybufs):
                y_ref[0, b, :, s * LANES:(s + 1) * LANES] = ybuf[pl.ds(b, tl, stride=nb), :]


def s5_scan(u_ctx, u_lat, bexp, cexp, ar, ai):
    b_, l_lat, w = u_lat.shape
    l_ctx = u_ctx.shape[1]
    tl = S5_CHUNK
    n_ctx, n_lat = l_ctx // tl, l_lat // tl
    ncb = w // S5_COLS
    ns = bexp.shape[-1]

    def ctx_idx(d, i):
        return jnp.where(d == 0, jnp.minimum(i, n_ctx - 1), jnp.maximum(n_ctx - 1 - i, 0))

    def lat_idx(d, i):
        k = jnp.maximum(i - n_ctx, 0)
        return jnp.where(d == 0, k, n_lat - 1 - k)

    return pl.pallas_call(
        functools.partial(_s5_kernel, n_ctx=n_ctx),
        grid=(2, ncb, n_ctx + n_lat),
        in_specs=[pl.BlockSpec((b_, tl, S5_COLS), lambda d, j, i: (0, ctx_idx(d, i), j)),
                  pl.BlockSpec((b_, tl, S5_COLS), lambda d, j, i: (0, lat_idx(d, i), j)),
                  pl.BlockSpec((1, 1, S5_COLS, ns), lambda d, j, i: (d, j, 0, 0)),
                  pl.BlockSpec((1, 1, ns, S5_COLS), lambda d, j, i: (d, j, 0, 0)),
                  pl.BlockSpec((1, 1, b_, ns // 2), lambda d, j, i: (d, j, 0, 0)),
                  pl.BlockSpec((1, 1, b_, ns // 2), lambda d, j, i: (d, j, 0, 0))],
        out_specs=pl.BlockSpec((1, b_, tl, S5_COLS), lambda d, j, i: (d, 0, lat_idx(d, i), j)),
        out_shape=jax.ShapeDtypeStruct((2, b_, l_lat, w), F32),
        scratch_shapes=([pltpu.VMEM((tl * b_, LANES), F32)] * (2 * (S5_COLS // LANES))
                        + [pltpu.VMEM((tl * b_, ns), F32), pltpu.VMEM((b_, ns), F32)]),
        compiler_params=_cparams(("parallel", "parallel", "arbitrary")),
        name="s5_scan",
    )(u_ctx, u_lat, bexp, cexp, ar, ai)


def _s5_expand(lam_re, lam_im, log_dt, b_re, b_im, c_re, c_im, batch):
    lam = lax.complex(lam_re.astype(F32), lam_im.astype(F32))
    step = jnp.exp(log_dt.astype(F32))[:, None]
    a_bar = jnp.exp(lam * step)
    b_mat = lax.complex(b_re.astype(F32), b_im.astype(F32))
    b_bar = ((a_bar - 1.0) / lam)[:, :, None] * b_mat
    g, p, h = b_bar.shape
    gl = S5_COLS // h
    ncb = g // gl
    eye = jnp.eye(gl, dtype=F32)

    def expand_b(t):
        return jnp.einsum('jgph,gk->jghkp', t.reshape(ncb, gl, p, h), eye).reshape(ncb, gl * h, gl * p)

    def expand_c(t):
        return jnp.einsum('jghp,gk->jgpkh', t.reshape(ncb, gl, h, p), eye).reshape(ncb, gl * p, gl * h)

    bexp = jnp.concatenate([expand_b(jnp.real(b_bar)), expand_b(jnp.imag(b_bar))], axis=-1)
    cexp = jnp.concatenate([expand_c(c_re.astype(F32)), -expand_c(c_im.astype(F32))], axis=1)
    ar = jnp.broadcast_to(jnp.real(a_bar).reshape(ncb, 1, gl * p), (ncb, batch, gl * p))
    ai = jnp.broadcast_to(jnp.imag(a_bar).reshape(ncb, 1, gl * p), (ncb, batch, gl * p))
    return bexp.astype(BF16), cexp.astype(BF16), ar, ai


def _s5_glu_kernel(yf_ref, yb_ref, u_ref, d_ref, gw_ref, gb_ref, wa_ref, o_ref):
    y = yf_ref[0, 0] + yb_ref[0, 0] + d_ref[...] * u_ref[0]
    g = _gelu_tanh(y)
    gate = _sigmoid(_dot(g.astype(BF16), gw_ref[...]) + gb_ref[...])
    o_ref[0] = _dot((g * gate).astype(BF16), wa_ref[...])


def s5_glu_branch(y2, u, s5_d, glu_w, glu_b, w_branch):
    b_, l, w = u.shape
    dm = w_branch.shape[1]
    tm = 256
    row = lambda: pl.BlockSpec((1, w), lambda b, i: (0, 0))
    return pl.pallas_call(
        _s5_glu_kernel,
        grid=(b_, l // tm),
        in_specs=[pl.BlockSpec((1, 1, tm, w), lambda b, i: (0, b, i, 0)),
                  pl.BlockSpec((1, 1, tm, w), lambda b, i: (1, b, i, 0)),
                  pl.BlockSpec((1, tm, w), lambda b, i: (b, i, 0)),
                  row(),
                  pl.BlockSpec((w, w), lambda b, i: (0, 0)),
                  row(),
                  pl.BlockSpec((w, dm), lambda b, i: (0, 0))],
        out_specs=pl.BlockSpec((1, tm, dm), lambda b, i: (b, i, 0)),
        out_shape=jax.ShapeDtypeStruct((b_, l, dm), F32),
        compiler_params=_cparams(("parallel", "arbitrary")),
        name="s5_glu_branch",
    )(y2, y2, u, s5_d.reshape(1, w), glu_w, glu_b.reshape(1, w), w_branch)


def _ssd_prep_kernel(prev_ref, cur_ref, next_ref, dt_ref, cw_ref, cb_ref, dtb_ref, xo_ref, dto_ref, *, kw):
    i = pl.program_id(1)
    last = pl.num_programs(1) - 1
    tl = cur_ref.shape[1]
    halo = prev_ref.shape[1]
    pad = kw // 2
    prev = jnp.where(i == 0, 0.0, prev_ref[0])
    nxt = jnp.where(i == last, 0.0, next_ref[0])
    ext = jnp.concatenate([prev, cur_ref[0], nxt], axis=0)
    acc = cb_ref[...] + jnp.zeros((tl, cur_ref.shape[2]), F32)
    for k in range(kw):
        s0 = halo + k - pad
        acc = acc + ext[s0:s0 + tl, :] * cw_ref[k:k + 1, :]
    xo_ref[0] = _silu(acc)
    dto_ref[0] = _softplus(dt_ref[0] + dtb_ref[...])


def ssd_prep(xbc, dt_raw, conv_w, conv_b, dt_bias_row):
    b_, l, c = xbc.shape
    tl = min(256, l)
    halo = 8
    hb = tl // halo
    nh = l // halo
    kw = conv_w.shape[0]
    cw = jnp.zeros((8, c), F32).at[:kw].set(conv_w)
    dp = dt_raw.shape[2]
    return pl.pallas_call(
        functools.partial(_ssd_prep_kernel, kw=kw),
        grid=(b_, l // tl),
        in_specs=[pl.BlockSpec((1, halo, c), lambda b, i: (b, jnp.maximum(i * hb - 1, 0), 0)),
                  pl.BlockSpec((1, tl, c), lambda b, i: (b, i, 0)),
                  pl.BlockSpec((1, halo, c), lambda b, i: (b, jnp.minimum((i + 1) * hb, nh - 1), 0)),
                  pl.BlockSpec((1, tl, dp), lambda b, i: (b, i, 0)),
                  pl.BlockSpec((8, c), lambda b, i: (0, 0)),
                  pl.BlockSpec((1, c), lambda b, i: (0, 0)),
                  pl.BlockSpec((1, dp), lambda b, i: (0, 0))],
        out_specs=[pl.BlockSpec((1, tl, c), lambda b, i: (b, i, 0)),
                   pl.BlockSpec((1, tl, dp), lambda b, i: (b, i, 0))],
        out_shape=[jax.ShapeDtypeStruct((b_, l, c), F32),
                   jax.ShapeDtypeStruct((b_, l, dp), F32)],
        compiler_params=_cparams(("parallel", "arbitrary")),
        name="ssd_prep",
    )(xbc, xbc, xbc, dt_raw, cw, conv_b.reshape(1, c), dt_bias_row)


def _ssd_scan_kernel(xc_ref, xl_ref, dtc_ref, dtl_ref, a_ref, y_ref, state,
                     *, n_ctx, reverse, width, col0):
    i = pl.program_id(1)
    q = xl_ref.shape[1]
    n = SSD_STATE
    p = SSD_HEADDIM
    ng = SSD_GROUPS
    gw = width // ng
    hpg = gw // p
    is_ctx = i < n_ctx

    @pl.when(i == 0)
    def _():
        state[...] = jnp.zeros_like(state)

    xc = jnp.where(is_ctx, xc_ref[0], xl_ref[0])
    dt = jnp.where(is_ctx, dtc_ref[0], dtl_ref[0])
    dta = dt * a_ref[...]
    ri = lax.broadcasted_iota(jnp.int32, (q, q), 0)
    ci = lax.broadcasted_iota(jnp.int32, (q, q), 1)
    mask = (ri <= ci) if reverse else (ri >= ci)
    tri = mask.astype(BF16)
    hi, mid, lo = _split3(dta)
    acum = _dot(tri, hi) + _dot(tri, mid) + _dot(tri, lo)
    total = jnp.sum(dta, axis=0, keepdims=True)
    acum_t = acum[:, :128].T
    dt_t = dt[:, :128].T

    def group_common(g):
        bmat = xc[:, width + g * n: width + (g + 1) * n]
        bt = bmat.T.astype(BF16)
        xg = xc[:, g * gw:(g + 1) * gw]
        return bt, xg

    for g in range(ng):
        bt, xg = group_common(g)
        parts, decs = [], []
        for r in range(hpg):
            hc = col0 + g * hpg + r
            col = acum[:, hc:hc + 1]
            tot = total[:, hc:hc + 1]
            wend = jnp.exp(tot - col) * dt[:, hc:hc + 1]
            parts.append(xg[:, r * p:(r + 1) * p] * wend)
            decs.append(jnp.broadcast_to(jnp.exp(tot), (1, p)))
        xw = jnp.concatenate(parts, axis=1).astype(BF16)
        dec = jnp.concatenate(decs, axis=1)

        @pl.when(jnp.logical_not(is_ctx))
        def _(g=g, bt=bt, xg=xg):
            cmat = xc[:, width + ng * n + g * n: width + ng * n + (g + 1) * n].astype(BF16)
            cb = _dot(cmat, bt)
            yoff = _dot(cmat, state[g].astype(BF16))
            ys = []
            for r in range(hpg):
                hc = col0 + g * hpg + r
                col = acum[:, hc:hc + 1]
                diff = col - acum_t[hc:hc + 1, :]
                lm = jnp.where(mask, jnp.exp(jnp.where(mask, diff, 0.0)), 0.0) * dt_t[hc:hc + 1, :]
                gm = (cb * lm).astype(BF16)
                yd = _dot(gm, xg[:, r * p:(r + 1) * p].astype(BF16))
                ys.append(yd + yoff[:, r * p:(r + 1) * p] * jnp.exp(col))
            y_ref[0, :, g * gw:(g + 1) * gw] = jnp.concatenate(ys, axis=1)

        state[g] = state[g] * dec + _dot(bt, xw)


def ssd_scan(xc_ctx, xc_lat, dt_ctx, dt_lat, a_row, *, reverse, width):
    b_, l_lat, c = xc_lat.shape
    l_ctx = xc_ctx.shape[1]
    q = SSD_CHUNK
    n_ctx, n_lat = l_ctx // q, l_lat // q
    dp = dt_lat.shape[2]
    heads = width // SSD_HEADDIM
    col0 = heads if reverse else 0

    def ctx_idx(i):
        return jnp.maximum(n_ctx - 1 - i, 0) if reverse else jnp.minimum(i, n_ctx - 1)

    def lat_idx(i):
        k = jnp.maximum(i - n_ctx, 0)
        return n_lat - 1 - k if reverse else k

    return pl.pallas_call(
        functools.partial(_ssd_scan_kernel, n_ctx=n_ctx, reverse=reverse, width=width, col0=col0),
        grid=(b_, n_ctx + n_lat),
        in_specs=[pl.BlockSpec((1, q, c), lambda b, i: (b, ctx_idx(i), 0)),
                  pl.BlockSpec((1, q, c), lambda b, i: (b, lat_idx(i), 0)),
                  pl.BlockSpec((1, q, dp), lambda b, i: (b, ctx_idx(i), 0)),
                  pl.BlockSpec((1, q, dp), lambda b, i: (b, lat_idx(i), 0)),
                  pl.BlockSpec((1, dp), lambda b, i: (0, 0))],
        out_specs=pl.BlockSpec((1, q, width), lambda b, i: (b, lat_idx(i), 0)),
        out_shape=jax.ShapeDtypeStruct((b_, l_lat, width), F32),
        scratch_shapes=[pltpu.VMEM((SSD_GROUPS, SSD_STATE, width // SSD_GROUPS), F32)],
        compiler_params=_cparams(("parallel", "arbitrary")),
        name="ssd_scan_bwd" if reverse else "ssd_scan_fwd",
    )(xc_ctx, xc_lat, dt_ctx, dt_lat, a_row)


def _ssd_out_kernel(yf_ref, yb_ref, x_ref, z_ref, d_ref, nw_ref, wb_ref, o_ref):
    y = (yf_ref[0] + yb_ref[0] + d_ref[...] * x_ref[0]) * _silu(z_ref[0])
    w = y.shape[1]
    gw = w // SSD_GROUPS
    parts = [_rms(y[:, g * gw:(g + 1) * gw]) for g in range(SSD_GROUPS)]
    yn = (jnp.concatenate(parts, axis=1) * nw_ref[...]).astype(BF16)
    o_ref[0] = _dot(yn, wb_ref[...])


def ssd_out_branch(yf, yb, xc_lat, z, d_row, norm_w, w_branch):
    b_, l, w = yf.shape
    dm = w_branch.shape[1]
    tm = 256
    blk = lambda: pl.BlockSpec((1, tm, w), lambda b, i: (b, i, 0))
    row = lambda: pl.BlockSpec((1, w), lambda b, i: (0, 0))
    return pl.pallas_call(
        _ssd_out_kernel,
        grid=(b_, l // tm),
        in_specs=[blk(), blk(), blk(), blk(), row(), row(),
                  pl.BlockSpec((w, dm), lambda b, i: (0, 0))],
        out_specs=pl.BlockSpec((1, tm, dm), lambda b, i: (b, i, 0)),
        out_shape=jax.ShapeDtypeStruct((b_, l, dm), F32),
        compiler_params=_cparams(("parallel", "arbitrary")),
        name="ssd_out_branch",
    )(yf, yb, xc_lat, z, d_row, norm_w.reshape(1, w), w_branch)


def _merge_kernel(x_ref, ga_ref, gb_ref, ba_ref, bb_ref, gm_ref, wo_ref, o_ref, m_ref, *, d_model):
    r = x_ref.shape[1]
    k = x_ref.shape[2] // d_model
    for c in range(k):
        sl = slice(c * d_model, (c + 1) * d_model)
        m = (_sigmoid(ga_ref[0, :, sl]) * ba_ref[0, :, sl]
             + _sigmoid(gb_ref[0, :, sl]) * bb_ref[0, c * r:(c + 1) * r, :])
        m_ref[c * r:(c + 1) * r, :] = m.astype(BF16)
    out = _dot(m_ref[...], wo_ref[...])
    gm = gm_ref[0]
    for c in range(k):
        sl = slice(c * d_model, (c + 1) * d_model)
        o_ref[0, :, sl] = x_ref[0, :, sl] + gm * out[c * r:(c + 1) * r, :]


def merge(x, ga, gb, branch_a, branch_b_cm, gate_m, w_out):
    b_, l, d = x.shape
    rows = l // GRID_W
    k = 8
    view = lambda t: t.reshape(b_, rows, GRID_W * d)
    vspec = lambda: pl.BlockSpec((1, rows, k * d), lambda b, i: (b, 0, i))
    out = pl.pallas_call(
        functools.partial(_merge_kernel, d_model=d),
        grid=(b_, GRID_W // k),
        in_specs=[vspec(), vspec(), vspec(), vspec(),
                  pl.BlockSpec((1, k * rows, d), lambda b, i: (b, i, 0)),
                  pl.BlockSpec((1, 1, d), lambda b, i: (b, 0, 0)),
                  pl.BlockSpec((d, d), lambda b, i: (0, 0))],
        out_specs=vspec(),
        out_shape=jax.ShapeDtypeStruct((b_, rows, GRID_W * d), F32),
        scratch_shapes=[pltpu.VMEM((k * rows, d), BF16)],
        compiler_params=_cparams(("parallel", "arbitrary")),
        name="merge",
    )(view(x), view(ga), view(gb), view(branch_a), branch_b_cm, gate_m, w_out)
    return out.reshape(b_, l, d)


def _ffn_norm_router_kernel(h_ref, nw_ref, sc_ref, sh_ref, rw_ref, rb_ref, xn_ref, lg_ref):
    u = (_rms(h_ref[0]) * nw_ref[...]) * (1.0 + sc_ref[0]) + sh_ref[0]
    xn_ref[0] = u.astype(BF16)
    uh, um, ul = _split3(u)
    wh, wm, wl = rw_ref[0], rw_ref[1], rw_ref[2]
    lg = (_dot(uh, wh) + (_dot(uh, wm) + _dot(um, wh))
          + (_dot(uh, wl) + _dot(um, wm) + _dot(ul, wh)))
    lg_ref[0] = lg + rb_ref[...]


def ffn_norm_router(h, norm_w, scale, shift, router_w3, router_b):
    b_, l, d = h.shape
    tm = 512
    rp = router_w3.shape[2]
    vec = lambda: pl.BlockSpec((1, 1, d), lambda b, i: (b, 0, 0))
    return pl.pallas_call(
        _ffn_norm_router_kernel,
        grid=(b_, l // tm),
        in_specs=[pl.BlockSpec((1, tm, d), lambda b, i: (b, i, 0)),
                  pl.BlockSpec((1, d), lambda b, i: (0, 0)),
                  vec(), vec(),
                  pl.BlockSpec((3, d, rp), lambda b, i: (0, 0, 0)),
                  pl.BlockSpec((1, rp), lambda b, i: (0, 0))],
        out_specs=[pl.BlockSpec((1, tm, d), lambda b, i: (b, i, 0)),
                   pl.BlockSpec((1, tm, rp), lambda b, i: (b, i, 0))],
        out_shape=[jax.ShapeDtypeStruct((b_, l, d), BF16),
                   jax.ShapeDtypeStruct((b_, l, rp), F32)],
        compiler_params=_cparams(("parallel", "arbitrary")),
        name="ffn_norm_router",
    )(h, norm_w.reshape(1, d), scale, shift, router_w3, router_b)


def _expert_kernel(be_ref, nu_ref, x_ref, wg_ref, bg_ref, wu_ref, bu_ref, wd_ref, bd_ref, sg_ref, o_ref):
    i = pl.program_id(0)

    @pl.when(i < nu_ref[0])
    def _():
        xb = x_ref[...]
        gate = jnp.minimum(_dot(xb, wg_ref[0]) + bg_ref[0], SWIGLU_LIMIT)
        up = jnp.clip(_dot(xb, wu_ref[0]) + bu_ref[0], -SWIGLU_LIMIT, SWIGLU_LIMIT)
        act = (up + 1.0) * gate * _sigmoid(SWIGLU_ALPHA * gate)
        y = _dot(act.astype(BF16), wd_ref[0]) + bd_ref[0]
        o_ref[...] = y * sg_ref[...]

    @pl.when(i >= nu_ref[0])
    def _():
        o_ref[...] = jnp.zeros_like(o_ref)


def expert_blocks(block_expert, n_used, x_sorted, wg, bg, wu, bu, wd, bd, slot_gate):
    n_slots, d = x_sorted.shape
    de = wg.shape[2]
    nblk = n_slots // MOE_BLOCK
    wspec = lambda a, b: pl.BlockSpec((1, a, b), lambda i, be, nu: (be[i], 0, 0))
    grid_spec = pltpu.PrefetchScalarGridSpec(
        num_scalar_prefetch=2,
        grid=(nblk,),
        in_specs=[pl.BlockSpec((MOE_BLOCK, d), lambda i, be, nu: (i, 0)),
                  wspec(d, de), wspec(1, de), wspec(d, de), wspec(1, de), wspec(de, d), wspec(1, d),
                  pl.BlockSpec((MOE_BLOCK, 1), lambda i, be, nu: (i, 0))],
        out_specs=pl.BlockSpec((MOE_BLOCK, d), lambda i, be, nu: (i, 0)),
    )
    return pl.pallas_call(
        _expert_kernel,
        grid_spec=grid_spec,
        out_shape=jax.ShapeDtypeStruct((n_slots, d), F32),
        compiler_params=_cparams(("arbitrary",)),
        name="expert_blocks",
    )(block_expert, n_used, x_sorted, wg, bg, wu, bu, wd, bd, slot_gate)


def _final_kernel(h_ref, f_ref, g_ref, w_ref, o_ref):
    h = h_ref[0] + g_ref[0] * f_ref[0]
    o_ref[0] = _rms(h) * w_ref[...]


def final_residual_norm(h, ffn, gate_f, final_w):
    b_, l, d = h.shape
    tm = 512
    blk = lambda: pl.BlockSpec((1, tm, d), lambda b, i: (b, i, 0))
    return pl.pallas_call(
        _final_kernel,
        grid=(b_, l // tm),
        in_specs=[blk(), blk(),
                  pl.BlockSpec((1, 1, d), lambda b, i: (b, 0, 0)),
                  pl.BlockSpec((1, d), lambda b, i: (0, 0))],
        out_specs=blk(),
        out_shape=jax.ShapeDtypeStruct((b_, l, d), F32),
        compiler_params=_cparams(("parallel", "arbitrary")),
        name="final_residual_norm",
    )(h, ffn, gate_f, final_w.reshape(1, d))


def _route(logits):
    n = logits.shape[0]
    top_val, top_idx = lax.top_k(logits, TOP_K)
    gates = jax.nn.softmax(top_val, axis=-1)
    n_pairs = n * TOP_K
    n_blocks = n_pairs // MOE_BLOCK + N_EXPERTS
    pair_expert = top_idx.reshape(n_pairs).astype(jnp.int32)
    order = jnp.argsort(pair_expert)
    sorted_expert = pair_expert[order]
    counts = jnp.bincount(pair_expert, length=N_EXPERTS).astype(jnp.int32)
    padded = (counts + MOE_BLOCK - 1) // MOE_BLOCK * MOE_BLOCK
    padded_end = jnp.cumsum(padded)
    rank = jnp.arange(n_pairs, dtype=jnp.int32) - (jnp.cumsum(counts) - counts)[sorted_expert]
    dest = (padded_end - padded)[sorted_expert] + rank
    n_slots = n_blocks * MOE_BLOCK
    slot_token = jnp.full((n_slots,), n, jnp.int32).at[dest].set((order // TOP_K).astype(jnp.int32))
    slot_gate = jnp.zeros((n_slots,), F32).at[dest].set(gates.reshape(n_pairs)[order])
    pair_slot = jnp.zeros((n_pairs,), jnp.int32).at[order].set(dest.astype(jnp.int32))
    block_start = jnp.arange(n_blocks, dtype=jnp.int32) * MOE_BLOCK
    block_expert = jnp.minimum(jnp.searchsorted(padded_end, block_start, side='right'),
                               N_EXPERTS - 1).astype(jnp.int32)
    n_used = (padded_end[-1] // MOE_BLOCK).astype(jnp.int32).reshape(1)
    return slot_token, slot_gate, pair_slot.reshape(n, TOP_K), block_expert, n_used


def kernel(x, c, ctx, c_ctx, ada_w, ada_b, norm_mix_w, w_in, s5_lam_re, s5_lam_im, s5_log_dt, s5_b_re, s5_b_im, s5_c_re, s5_c_im, s5_d, s5_glu_w, s5_glu_b, ssd_conv_w, ssd_conv_b, ssd_dt_bias, ssd_a_log, ssd_d, ssd_norm_w, w_branch_a, w_branch_b, w_out, norm_ffn_w, router_w, router_b, moe_w_gate, moe_b_gate, moe_w_up, moe_b_up, moe_w_down, moe_b_down, final_norm_w):
    assert ada_w.shape[0] == 1, "single-layer block"
    b_, l, d = x.shape
    s5_w = s5_d.shape[1]
    ssd_w = ssd_d.shape[1] * SSD_HEADDIM
    heads = ssd_d.shape[1]
    conv_dim = ssd_w + 2 * SSD_GROUPS * SSD_STATE
    c0, c1, c2, c3, c4 = (s5_w, s5_w + conv_dim, s5_w + conv_dim + 2 * heads,
                          s5_w + conv_dim + 2 * heads + ssd_w, s5_w + conv_dim + 2 * heads + ssd_w + d)

    cond = jnp.concatenate([c, c_ctx[None, :], jnp.zeros((7, d), F32)], axis=0)
    mod = ada_modulation(cond, ada_w[0], ada_b[0])
    mod6 = mod.reshape(mod.shape[0], 6, d)
    lat_mod = [mod6[:b_, k][:, None, :] for k in range(6)]
    ctx_mod = [jnp.broadcast_to(mod6[b_, k][None, None, :], (b_, 1, d)) for k in range(6)]
    sh_m, sc_m, g_m, sh_f, sc_f, g_f = lat_mod

    w = w_in[0]
    w_row = jnp.concatenate([w[:, :c0], w[:, c3:c4], w[:, c4:]], axis=1).astype(BF16)
    w_dt = jnp.zeros((d, DT_PAD), F32).at[:, :2 * heads].set(w[:, c1:c2])
    w_col = jnp.concatenate([w[:, c0:c1], w[:, c2:c3], w_dt], axis=1).astype(BF16)

    u_lat, ga, gb = in_projection(x, norm_mix_w[0], sc_m, sh_m, w_row, (s5_w, d, d),
                                  col_major=False, tokens_per_step=512)
    xbc_lat, z_lat, dtr_lat = in_projection(x, norm_mix_w[0], sc_m, sh_m, w_col, (conv_dim, ssd_w, DT_PAD),
                                            col_major=True, tokens_per_step=256)
    lc = ctx.shape[1]
    (u_ctx,) = in_projection(ctx, norm_mix_w[0], ctx_mod[1], ctx_mod[0], w_row[:, :s5_w], (s5_w,),
                             col_major=False, tokens_per_step=lc)
    w_col_ctx = jnp.concatenate([w_col[:, :conv_dim], w_col[:, conv_dim + ssd_w:]], axis=1)
    xbc_ctx, dtr_ctx = in_projection(ctx, norm_mix_w[0], ctx_mod[1], ctx_mod[0], w_col_ctx, (conv_dim, DT_PAD),
                                     col_major=False, tokens_per_step=lc)

    exp_f = _s5_expand(s5_lam_re[0, 0], s5_lam_im[0, 0], s5_log_dt[0, 0], s5_b_re[0, 0], s5_b_im[0, 0],
                       s5_c_re[0, 0], s5_c_im[0, 0], b_)
    exp_b = _s5_expand(s5_lam_re[0, 1], s5_lam_im[0, 1], s5_log_dt[0, 1], s5_b_re[0, 1], s5_b_im[0, 1],
                       s5_c_re[0, 1], s5_c_im[0, 1], b_)
    bexp, cexp, ar, ai = [jnp.stack([f, g]) for f, g in zip(exp_f, exp_b)]
    y_s5 = s5_scan(u_ctx, u_lat, bexp, cexp, ar, ai)
    branch_a = s5_glu_branch(y_s5, u_lat, s5_d[0], s5_glu_w[0].astype(BF16), s5_glu_b[0],
                             w_branch_a[0].astype(BF16))

    dt_bias_row = jnp.zeros((1, DT_PAD), F32).at[0, :2 * heads].set(ssd_dt_bias[0].reshape(-1))
    a_row = jnp.zeros((1, DT_PAD), F32).at[0, :2 * heads].set(-jnp.exp(ssd_a_log[0].astype(F32)).reshape(-1))
    xc_lat, dt_lat = ssd_prep(xbc_lat, dtr_lat, ssd_conv_w[0], ssd_conv_b[0], dt_bias_row)
    xc_ctx, dt_ctx = ssd_prep(xbc_ctx, dtr_ctx, ssd_conv_w[0], ssd_conv_b[0], dt_bias_row)
    y_f = ssd_scan(xc_ctx, xc_lat, dt_ctx, dt_lat, a_row, reverse=False, width=ssd_w)
    y_b = ssd_scan(xc_ctx, xc_lat, dt_ctx, dt_lat, a_row, reverse=True, width=ssd_w)
    d_row = jnp.repeat(ssd_d[0].astype(F32), SSD_HEADDIM).reshape(1, ssd_w)
    branch_b = ssd_out_branch(y_f, y_b, xc_lat, z_lat, d_row, ssd_norm_w[0], w_branch_b[0].astype(BF16))

    h1 = merge(x, ga, gb, branch_a, branch_b, g_m, w_out[0].astype(BF16))

    rw = jnp.zeros((d, ROUTER_PAD), F32).at[:, :N_EXPERTS].set(router_w[0])
    rw3 = jnp.stack(_split3(rw))
    rb = jnp.zeros((1, ROUTER_PAD), F32).at[0, :N_EXPERTS].set(router_b[0])
    xn, logits = ffn_norm_router(h1, norm_ffn_w[0], sc_f, sh_f, rw3, rb)
    n = b_ * l
    xn = xn.reshape(n, d)
    slot_token, slot_gate, pair_slot, block_expert, n_used = _route(logits.reshape(n, ROUTER_PAD)[:, :N_EXPERTS])
    x_pad = jnp.concatenate([xn, jnp.zeros((1, d), xn.dtype)], axis=0)
    x_sorted = x_pad[slot_token]
    y_sorted = expert_blocks(block_expert, n_used, x_sorted,
                             moe_w_gate[0].astype(BF16), moe_b_gate[0][:, None, :],
                             moe_w_up[0].astype(BF16), moe_b_up[0][:, None, :],
                             moe_w_down[0].astype(BF16), moe_b_down[0][:, None, :],
                             slot_gate[:, None])
    ffn = jnp.sum(y_sorted[pair_slot], axis=1).reshape(b_, l, d)

    return final_residual_norm(h1, ffn, g_f, final_norm_w)
```

```python
import functools
import math

import jax
import jax.numpy as jnp
from jax import lax
from jax.experimental import pallas as pl
from jax.experimental.pallas import tpu as pltpu

F32 = jnp.float32
BF16 = jnp.bfloat16

EPS = 1e-6
GRID_W = 64
LANES = 128

S5_GROUP = 16
S5_STATE = 64
S5_COLS = 256
S5_CHUNK = 128

SSD_HEADDIM = 64
SSD_GROUPS = 8
SSD_STATE = 128
SSD_CHUNK = 128
DT_PAD = 256
MASKED_EXPONENT = -1e30

N_EXPERTS = 32
TOP_K = 4
SWIGLU_ALPHA = 1.702
SWIGLU_LIMIT = 7.0
MOE_BLOCK = 256
ROUTER_PAD = 128

VMEM_LIMIT_BYTES = 56 * 1024 * 1024


def _cparams(sem):
    return pltpu.CompilerParams(dimension_semantics=sem, vmem_limit_bytes=VMEM_LIMIT_BYTES)


def _dot(a, b):
    return jnp.dot(a, b, preferred_element_type=F32)


def _split3(x):
    hi = x.astype(BF16)
    r = x - hi.astype(F32)
    mid = r.astype(BF16)
    lo = (r - mid.astype(F32)).astype(BF16)
    return hi, mid, lo


def _sigmoid(x):
    return 1.0 / (1.0 + jnp.exp(-x))


def _silu(x):
    return x * _sigmoid(x)


def _softplus(x):
    return jnp.maximum(x, 0.0) + jnp.log1p(jnp.exp(-jnp.abs(x)))


def _gelu_tanh(x):
    c = math.sqrt(2.0 / math.pi)
    return 0.5 * x * (1.0 + jnp.tanh(c * (x + 0.044715 * (x * x * x))))


def _rms(x):
    return x * lax.rsqrt(jnp.mean(x * x, axis=-1, keepdims=True) + EPS)


def _ada_kernel(c_ref, w_ref, b_ref, o_ref):
    s = _silu(c_ref[...]).astype(BF16)
    o_ref[...] = _dot(s, w_ref[...].astype(BF16)) + b_ref[...]


def ada_modulation(cond, w, b):
    n, d = cond.shape
    cols = w.shape[1]
    tn = 1024
    return pl.pallas_call(
        _ada_kernel,
        grid=(cols // tn,),
        in_specs=[pl.BlockSpec((n, d), lambda j: (0, 0)),
                  pl.BlockSpec((d, tn), lambda j: (0, j)),
                  pl.BlockSpec((1, tn), lambda j: (0, j))],
        out_specs=pl.BlockSpec((n, tn), lambda j: (0, j)),
        out_shape=jax.ShapeDtypeStruct((n, cols), F32),
        compiler_params=_cparams(("arbitrary",)),
        name="ada_modulation",
    )(cond, w, b.reshape(1, cols))


def _inproj_kernel(x_ref, nw_ref, sc_ref, sh_ref, w_ref, *rest, d_model, widths):
    o_refs = rest[:len(widths)]
    u_ref = rest[len(widths)]
    r = x_ref.shape[1]
    k = x_ref.shape[2] // d_model
    nw = nw_ref[...]
    sc = 1.0 + sc_ref[0]
    sh = sh_ref[0]
    for c in range(k):
        xs = x_ref[0, :, c * d_model:(c + 1) * d_model]
        u_ref[c * r:(c + 1) * r, :] = ((_rms(xs) * nw) * sc + sh).astype(BF16)
    off = 0
    for o_ref, wd in zip(o_refs, widths):
        step = min(wd, 512)
        for n0 in range(0, wd, step):
            o_ref[0, :, n0:n0 + step] = _dot(u_ref[...], w_ref[:, off + n0:off + n0 + step]).astype(o_ref.dtype)
        off += wd


def in_projection(x, norm_w, scale, shift, w, widths, *, col_major, tokens_per_step):
    b_, l, d = x.shape
    if col_major:
        rows = l // GRID_W
        k = tokens_per_step // rows
        xv = x.reshape(b_, rows, GRID_W * d)
        x_spec = pl.BlockSpec((1, rows, k * d), lambda b, i: (b, 0, i))
        steps = GRID_W // k
    else:
        xv = x
        x_spec = pl.BlockSpec((1, tokens_per_step, d), lambda b, i: (b, i, 0))
        steps = l // tokens_per_step
    vec = lambda: pl.BlockSpec((1, 1, d), lambda b, i: (b, 0, 0))
    return pl.pallas_call(
        functools.partial(_inproj_kernel, d_model=d, widths=tuple(widths)),
        grid=(b_, steps),
        in_specs=[x_spec,
                  pl.BlockSpec((1, d), lambda b, i: (0, 0)),
                  vec(), vec(),
                  pl.BlockSpec(w.shape, lambda b, i: (0, 0), pipeline_mode=pl.Buffered(1))],
        out_specs=[pl.BlockSpec((1, tokens_per_step, wd), lambda b, i: (b, i, 0)) for wd in widths],
        out_shape=[jax.ShapeDtypeStruct((b_, l, wd), F32) for wd in widths],
        scratch_shapes=[pltpu.VMEM((tokens_per_step, d), BF16)],
        compiler_params=_cparams(("parallel", "arbitrary")),
        name="in_projection_col" if col_major else "in_projection_row",
    )(xv, norm_w.reshape(1, d), scale, shift, w)


def _s5_kernel(uc_ref, ul_ref, bexp_ref, cexp_ref, ar_ref, ai_ref, y_ref, *scratch, n_ctx):
    nslab = (len(scratch) - 2) // 2
    ubufs, ybufs = scratch[:nslab], scratch[nslab:2 * nslab]
    bu, state = scratch[2 * nslab:]
    d = pl.program_id(0)
    i = pl.program_id(2)
    nb, tl = ul_ref.shape[0], ul_ref.shape[1]
    half = state.shape[1] // 2
    is_ctx = i < n_ctx

    @pl.when(i == 0)
    def _():
        state[...] = jnp.zeros_like(state)

    for b in range(nb):
        ub = jnp.where(is_ctx, uc_ref[b], ul_ref[b])
        for s, ubuf in enumerate(ubufs):
            ubuf[pl.ds(b, tl, stride=nb), :] = ub[:, s * LANES:(s + 1) * LANES]
    u_tm = jnp.concatenate([ubuf[...] for ubuf in ubufs], axis=1)
    bu[...] = _dot(u_tm.astype(BF16), bexp_ref[0, 0])

    ar = ar_ref[0, 0]
    ai = ai_ref[0, 0]

    def step(t, carry):
        xr, xi = carry
        tok = jnp.where(d == 0, t, tl - 1 - t)
        row = pl.multiple_of(tok * nb, nb)
        br = bu[pl.ds(row, nb), 0:half]
        bi = bu[pl.ds(row, nb), half:2 * half]
        nxr = ar * xr - ai * xi + br
        nxi = ar * xi + ai * xr + bi
        bu[pl.ds(row, nb), 0:half] = nxr
        bu[pl.ds(row, nb), half:2 * half] = nxi
        return nxr, nxi

    xr, xi = lax.fori_loop(0, tl, step, (state[:, 0:half], state[:, half:2 * half]), unroll=2)
    state[:, 0:half] = xr
    state[:, half:2 * half] = xi

    @pl.when(jnp.logical_not(is_ctx))
    def _():
        y_tm = _dot(bu[...].astype(BF16), cexp_ref[0, 0])
        for s, ybuf in enumerate(ybufs):
            ybuf[...] = y_tm[:, s * LANES:(s + 1) * LANES]
        for b in range(nb):
            for s, ybuf in enumerate(ybufs):
                y_ref[0, b, :, s * LANES:(s + 1) * LANES] = ybuf[pl.ds(b, tl, stride=nb), :]


def s5_scan(u_ctx, u_lat, bexp, cexp, ar, ai):
    b_, l_lat, w = u_lat.shape
    l_ctx = u_ctx.shape[1]
    tl = S5_CHUNK
    n_ctx, n_lat = l_ctx // tl, l_lat // tl
    ncb = w // S5_COLS
    ns = bexp.shape[-1]

    def ctx_idx(d, i):
        return jnp.where(d == 0, jnp.minimum(i, n_ctx - 1), jnp.maximum(n_ctx - 1 - i, 0))

    def lat_idx(d, i):
        k = jnp.maximum(i - n_ctx, 0)
        return jnp.where(d == 0, k, n_lat - 1 - k)

    return pl.pallas_call(
        functools.partial(_s5_kernel, n_ctx=n_ctx),
        grid=(2, ncb, n_ctx + n_lat),
        in_specs=[pl.BlockSpec((b_, tl, S5_COLS), lambda d, j, i: (0, ctx_idx(d, i), j)),
                  pl.BlockSpec((b_, tl, S5_COLS), lambda d, j, i: (0, lat_idx(d, i), j)),
                  pl.BlockSpec((1, 1, S5_COLS, ns), lambda d, j, i: (d, j, 0, 0)),
                  pl.BlockSpec((1, 1, ns, S5_COLS), lambda d, j, i: (d, j, 0, 0)),
                  pl.BlockSpec((1, 1, b_, ns // 2), lambda d, j, i: (d, j, 0, 0)),
                  pl.BlockSpec((1, 1, b_, ns // 2), lambda d, j, i: (d, j, 0, 0))],
        out_specs=pl.BlockSpec((1, b_, tl, S5_COLS), lambda d, j, i: (d, 0, lat_idx(d, i), j)),
        out_shape=jax.ShapeDtypeStruct((2, b_, l_lat, w), F32),
        scratch_shapes=([pltpu.VMEM((tl * b_, LANES), F32)] * (2 * (S5_COLS // LANES))
                        + [pltpu.VMEM((tl * b_, ns), F32), pltpu.VMEM((b_, ns), F32)]),
        compiler_params=_cparams(("parallel", "parallel", "arbitrary")),
        name="s5_scan",
    )(u_ctx, u_lat, bexp, cexp, ar, ai)


def _s5_expand(lam_re, lam_im, log_dt, b_re, b_im, c_re, c_im, batch):
    lam = lax.complex(lam_re.astype(F32), lam_im.astype(F32))
    step = jnp.exp(log_dt.astype(F32))[:, None]
    a_bar = jnp.exp(lam * step)
    b_mat = lax.complex(b_re.astype(F32), b_im.astype(F32))
    b_bar = ((a_bar - 1.0) / lam)[:, :, None] * b_mat
    g, p, h = b_bar.shape
    gl = S5_COLS // h
    ncb = g // gl
    eye = jnp.eye(gl, dtype=F32)

    def expand_b(t):
        return jnp.einsum('jgph,gk->jghkp', t.reshape(ncb, gl, p, h), eye).reshape(ncb, gl * h, gl * p)

    def expand_c(t):
        return jnp.einsum('jghp,gk->jgpkh', t.reshape(ncb, gl, h, p), eye).reshape(ncb, gl * p, gl * h)

    bexp = jnp.concatenate([expand_b(jnp.real(b_bar)), expand_b(jnp.imag(b_bar))], axis=-1)
    cexp = jnp.concatenate([expand_c(c_re.astype(F32)), -expand_c(c_im.astype(F32))], axis=1)
    ar = jnp.broadcast_to(jnp.real(a_bar).reshape(ncb, 1, gl * p), (ncb, batch, gl * p))
    ai = jnp.broadcast_to(jnp.imag(a_bar).reshape(ncb, 1, gl * p), (ncb, batch, gl * p))
    return bexp.astype(BF16), cexp.astype(BF16), ar, ai


def _s5_glu_kernel(yf_ref, yb_ref, u_ref, d_ref, gw_ref, gb_ref, wa_ref, o_ref):
    y = yf_ref[0, 0] + yb_ref[0, 0] + d_ref[...] * u_ref[0]
    g = _gelu_tanh(y)
    gate = _sigmoid(_dot(g.astype(BF16), gw_ref[...]) + gb_ref[...])
    o_ref[0] = _dot((g * gate).astype(BF16), wa_ref[...])


def s5_glu_branch(y2, u, s5_d, glu_w, glu_b, w_branch):
    b_, l, w = u.shape
    dm = w_branch.shape[1]
    tm = 256
    row = lambda: pl.BlockSpec((1, w), lambda b, i: (0, 0))
    return pl.pallas_call(
        _s5_glu_kernel,
        grid=(b_, l // tm),
        in_specs=[pl.BlockSpec((1, 1, tm, w), lambda b, i: (0, b, i, 0)),
                  pl.BlockSpec((1, 1, tm, w), lambda b, i: (1, b, i, 0)),
                  pl.BlockSpec((1, tm, w), lambda b, i: (b, i, 0)),
                  row(),
                  pl.BlockSpec((w, w), lambda b, i: (0, 0)),
                  row(),
                  pl.BlockSpec((w, dm), lambda b, i: (0, 0))],
        out_specs=pl.BlockSpec((1, tm, dm), lambda b, i: (b, i, 0)),
        out_shape=jax.ShapeDtypeStruct((b_, l, dm), F32),
        compiler_params=_cparams(("parallel", "arbitrary")),
        name="s5_glu_branch",
    )(y2, y2, u, s5_d.reshape(1, w), glu_w, glu_b.reshape(1, w), w_branch)


def _ssd_prep_kernel(prev_ref, cur_ref, next_ref, dt_ref, cw_ref, cb_ref, dtb_ref, xo_ref, dto_ref, *, kw):
    i = pl.program_id(1)
    last = pl.num_programs(1) - 1
    tl = cur_ref.shape[1]
    halo = prev_ref.shape[1]
    pad = kw // 2
    prev = jnp.where(i == 0, 0.0, prev_ref[0])
    nxt = jnp.where(i == last, 0.0, next_ref[0])
    ext = jnp.concatenate([prev, cur_ref[0], nxt], axis=0)
    acc = cb_ref[...] + jnp.zeros((tl, cur_ref.shape[2]), F32)
    for k in range(kw):
        s0 = halo + k - pad
        acc = acc + ext[s0:s0 + tl, :] * cw_ref[k:k + 1, :]
    xo_ref[0] = _silu(acc)
    dto_ref[0] = _softplus(dt_ref[0] + dtb_ref[...])


def ssd_prep(xbc, dt_raw, conv_w, conv_b, dt_bias_row):
    b_, l, c = xbc.shape
    tl = min(256, l)
    halo = 8
    hb = tl // halo
    nh = l // halo
    kw = conv_w.shape[0]
    cw = jnp.zeros((8, c), F32).at[:kw].set(conv_w)
    dp = dt_raw.shape[2]
    return pl.pallas_call(
        functools.partial(_ssd_prep_kernel, kw=kw),
        grid=(b_, l // tl),
        in_specs=[pl.BlockSpec((1, halo, c), lambda b, i: (b, jnp.maximum(i * hb - 1, 0), 0)),
                  pl.BlockSpec((1, tl, c), lambda b, i: (b, i, 0)),
                  pl.BlockSpec((1, halo, c), lambda b, i: (b, jnp.minimum((i + 1) * hb, nh - 1), 0)),
                  pl.BlockSpec((1, tl, dp), lambda b, i: (b, i, 0)),
                  pl.BlockSpec((8, c), lambda b, i: (0, 0)),
                  pl.BlockSpec((1, c), lambda b, i: (0, 0)),
                  pl.BlockSpec((1, dp), lambda b, i: (0, 0))],
        out_specs=[pl.BlockSpec((1, tl, c), lambda b, i: (b, i, 0)),
                   pl.BlockSpec((1, tl, dp), lambda b, i: (b, i, 0))],
        out_shape=[jax.ShapeDtypeStruct((b_, l, c), F32),
                   jax.ShapeDtypeStruct((b_, l, dp), F32)],
        compiler_params=_cparams(("parallel", "arbitrary")),
        name="ssd_prep",
    )(xbc, xbc, xbc, dt_raw, cw, conv_b.reshape(1, c), dt_bias_row)


def _ssd_scan_kernel(*refs, reverse, width, col0, with_output):
    if with_output:
        x_ref, dt_ref, a_ref, e_ref, s0_ref, y_ref, state, bt_scr = refs
    else:
        x_ref, dt_ref, a_ref, e_ref, so_ref, state, bt_scr = refs
    i = pl.program_id(1)
    q = x_ref.shape[1]
    n = SSD_STATE
    p = SSD_HEADDIM
    ng = SSD_GROUPS
    gw = width // ng
    hpg = gw // p

    @pl.when(i == 0)
    def _():
        state[...] = s0_ref[0] if with_output else jnp.zeros_like(state)

    dt = dt_ref[0][:, :LANES]
    dta = dt * a_ref[...]
    ri = lax.broadcasted_iota(jnp.int32, (q, q), 0)
    ci = lax.broadcasted_iota(jnp.int32, (q, q), 1)
    mask = (ri <= ci) if reverse else (ri >= ci)
    tri = mask.astype(BF16)
    hi, mid, lo = _split3(dta)
    acum = _dot(tri, hi) + _dot(tri, mid) + _dot(tri, lo)

    def expand(v):
        h_, m_, l_ = _split3(v)
        return _dot(jnp.concatenate([h_, m_], axis=1), e_ref[...]) + _dot(l_, e_ref[0:LANES, :])

    acum_e = expand(acum)
    dt_e = expand(dt)
    last = 0 if reverse else q - 1
    tot_e = acum_e[last:last + 1, :]
    xdt = x_ref[0, :, 0:width] * dt_e
    xw = (xdt * jnp.exp(tot_e - acum_e)).astype(BF16)
    dec_e = jnp.exp(tot_e)
    for g in range(ng):
        bt_scr[g] = x_ref[0, :, width + g * n: width + (g + 1) * n].T.astype(BF16)

    if with_output:
        xdt_bf = xdt.astype(BF16)
        eac = jnp.exp(acum_e)
        acum_t = acum.T
        head_of_lane = lax.broadcasted_iota(jnp.int32, (1, gw), 1) // p
        for g in range(ng):
            sl = slice(g * gw, (g + 1) * gw)
            cmat = x_ref[0, :, width + (ng + g) * n: width + (ng + g + 1) * n].astype(BF16)
            cb = _dot(cmat, bt_scr[g])
            yoff = _dot(cmat, state[g].astype(BF16)) * eac[:, sl]
            xg = xdt_bf[:, sl]
            gms, rhs = [], []
            for r in range(hpg):
                hc = col0 + g * hpg + r
                diff = acum[:, hc:hc + 1] - acum_t[hc:hc + 1, :]
                gms.append((cb * jnp.exp(jnp.where(mask, diff, MASKED_EXPONENT))).astype(BF16))
                rhs.append(jnp.where(head_of_lane == r, xg, jnp.zeros_like(xg)))
            yd = _dot(jnp.concatenate(gms, axis=1), jnp.concatenate(rhs, axis=0))
            y_ref[0, :, sl] = yd + yoff

    for g in range(ng):
        sl = slice(g * gw, (g + 1) * gw)
        state[g] = state[g] * dec_e[:, sl] + _dot(bt_scr[g], xw[:, sl])
    if not with_output:
        so_ref[0] = state[...]


def ssd_scan(xc, dt, a_row, init_state, *, reverse, width):
    b_, l, c = xc.shape
    q = SSD_CHUNK
    nc = l // q
    dp = dt.shape[2]
    heads = width // SSD_HEADDIM
    col0 = heads if reverse else 0
    with_output = init_state is not None
    gw = width // SSD_GROUPS
    col = lax.broadcasted_iota(jnp.int32, (2 * LANES, width), 0) % LANES
    expand_mat = (col == col0 + lax.broadcasted_iota(jnp.int32, (2 * LANES, width), 1) // SSD_HEADDIM).astype(BF16)
    chunk = (lambda i: nc - 1 - i) if reverse else (lambda i: i)
    state_spec = pl.BlockSpec((1, SSD_GROUPS, SSD_STATE, gw), lambda b, i: (b, 0, 0, 0))
    in_specs = [pl.BlockSpec((1, q, c), lambda b, i: (b, chunk(i), 0)),
                pl.BlockSpec((1, q, dp), lambda b, i: (b, chunk(i), 0)),
                pl.BlockSpec((1, LANES), lambda b, i: (0, 0)),
                pl.BlockSpec((2 * LANES, width), lambda b, i: (0, 0))]
    args = [xc, dt, a_row, expand_mat]
    if with_output:
        in_specs.append(state_spec)
        args.append(init_state)
        out_specs = pl.BlockSpec((1, q, width), lambda b, i: (b, chunk(i), 0))
        out_shape = jax.ShapeDtypeStruct((b_, l, width), F32)
    else:
        out_specs = state_spec
        out_shape = jax.ShapeDtypeStruct((b_, SSD_GROUPS, SSD_STATE, gw), F32)
    return pl.pallas_call(
        functools.partial(_ssd_scan_kernel, reverse=reverse, width=width, col0=col0, with_output=with_output),
        grid=(b_, nc),
        in_specs=in_specs,
        out_specs=out_specs,
        out_shape=out_shape,
        scratch_shapes=[pltpu.VMEM((SSD_GROUPS, SSD_STATE, gw), F32),
                        pltpu.VMEM((SSD_GROUPS, SSD_STATE, q), BF16)],
        compiler_params=_cparams(("parallel", "arbitrary")),
        name=("ssd_scan" if with_output else "ssd_state") + ("_bwd" if reverse else "_fwd"),
    )(*args)


def _ssd_out_kernel(yf_ref, yb_ref, x_ref, z_ref, d_ref, nw_ref, wb_ref, o_ref):
    y = (yf_ref[0] + yb_ref[0] + d_ref[...] * x_ref[0]) * _silu(z_ref[0])
    w = y.shape[1]
    gw = w // SSD_GROUPS
    parts = [_rms(y[:, g * gw:(g + 1) * gw]) for g in range(SSD_GROUPS)]
    yn = (jnp.concatenate(parts, axis=1) * nw_ref[...]).astype(BF16)
    o_ref[0] = _dot(yn, wb_ref[...])


def ssd_out_branch(yf, yb, xc_lat, z, d_row, norm_w, w_branch):
    b_, l, w = yf.shape
    dm = w_branch.shape[1]
    tm = 256
    blk = lambda: pl.BlockSpec((1, tm, w), lambda b, i: (b, i, 0))
    row = lambda: pl.BlockSpec((1, w), lambda b, i: (0, 0))
    return pl.pallas_call(
        _ssd_out_kernel,
        grid=(b_, l // tm),
        in_specs=[blk(), blk(), blk(), blk(), row(), row(),
                  pl.BlockSpec((w, dm), lambda b, i: (0, 0))],
        out_specs=pl.BlockSpec((1, tm, dm), lambda b, i: (b, i, 0)),
        out_shape=jax.ShapeDtypeStruct((b_, l, dm), F32),
        compiler_params=_cparams(("parallel", "arbitrary")),
        name="ssd_out_branch",
    )(yf, yb, xc_lat, z, d_row, norm_w.reshape(1, w), w_branch)


def _merge_router_kernel(x_ref, ga_ref, gb_ref, ba_ref, bb_ref, gm_ref, wo_ref, nw_ref, sc_ref, sh_ref,
                         rw_ref, rb_ref, tri_ref, h_ref, xn_ref, ri_ref, rg_ref, cnt_ref, carry):
    @pl.when(jnp.logical_and(pl.program_id(0) == 0, pl.program_id(1) == 0))
    def _():
        carry[...] = jnp.zeros_like(carry)

    m = _sigmoid(ga_ref[0]) * ba_ref[0] + _sigmoid(gb_ref[0]) * bb_ref[0]
    h = x_ref[0] + gm_ref[0] * _dot(m.astype(BF16), wo_ref[...])
    h_ref[0] = h
    u = (_rms(h) * nw_ref[...]) * (1.0 + sc_ref[0]) + sh_ref[0]
    xn_ref[0] = u.astype(BF16)
    uh, um, ul = _split3(u)
    wh, wm, wl = rw_ref[0], rw_ref[1], rw_ref[2]
    lg = (_dot(uh, wh) + (_dot(uh, wm) + _dot(um, wh))
          + (_dot(uh, wl) + _dot(um, wm) + _dot(ul, wh))) + rb_ref[...]

    tm, rp = lg.shape
    lane = lax.broadcasted_iota(jnp.int32, (tm, rp), 1)
    lane_f = lane.astype(F32)
    neg = jnp.float32(-jnp.inf)
    cur = jnp.where(lane < N_EXPERTS, lg, neg)
    onehots, vals, idxs = [], [], []
    for _ in range(TOP_K):
        mx = jnp.max(cur, axis=-1, keepdims=True)
        idx = jnp.min(jnp.where(cur == mx, lane_f, float(rp)), axis=-1, keepdims=True)
        oh = lane_f == idx
        onehots.append(oh)
        vals.append(mx)
        idxs.append(idx)
        cur = jnp.where(oh, neg, cur)
    exps = [jnp.exp(v - vals[0]) for v in vals]
    denom = exps[0] + exps[1] + exps[2] + exps[3]
    oh_all = (onehots[0] | onehots[1] | onehots[2] | onehots[3]).astype(F32)
    before = _dot(tri_ref[...], oh_all.astype(BF16)) + carry[...]
    ri = jnp.zeros((tm, rp), jnp.int32)
    rg = jnp.zeros((tm, rp), F32)
    for k in range(TOP_K):
        rank = jnp.sum(jnp.where(onehots[k], before, 0.0), axis=-1, keepdims=True).astype(jnp.int32)
        ri = jnp.where(lane == k, idxs[k].astype(jnp.int32), ri)
        ri = jnp.where(lane == TOP_K + k, rank, ri)
        rg = jnp.where(lane == k, exps[k] / denom, rg)
    ri_ref[0] = ri
    rg_ref[0] = rg
    carry[...] = carry[...] + jnp.sum(oh_all, axis=0, keepdims=True)
    cnt_ref[...] = jnp.broadcast_to(carry[...], cnt_ref.shape)


def merge_router(x, ga, gb, branch_a, branch_b, gate_m, w_out, norm_w, scale, shift, router_w3, router_b):
    b_, l, d = x.shape
    tm = 512
    rp = router_w3.shape[2]
    tri = (lax.broadcasted_iota(jnp.int32, (tm, tm), 0) > lax.broadcasted_iota(jnp.int32, (tm, tm), 1)).astype(BF16)
    blk = lambda w=d: pl.BlockSpec((1, tm, w), lambda b, i: (b, i, 0))
    vec = lambda: pl.BlockSpec((1, 1, d), lambda b, i: (b, 0, 0))
    return pl.pallas_call(
        _merge_router_kernel,
        grid=(b_, l // tm),
        in_specs=[blk(), blk(), blk(), blk(), blk(), vec(),
                  pl.BlockSpec((d, d), lambda b, i: (0, 0)),
                  pl.BlockSpec((1, d), lambda b, i: (0, 0)),
                  vec(), vec(),
                  pl.BlockSpec((3, d, rp), lambda b, i: (0, 0, 0)),
                  pl.BlockSpec((1, rp), lambda b, i: (0, 0)),
                  pl.BlockSpec((tm, tm), lambda b, i: (0, 0))],
        out_specs=[blk(), blk(), blk(rp), blk(rp),
                   pl.BlockSpec((8, rp), lambda b, i: (0, 0))],
        out_shape=[jax.ShapeDtypeStruct((b_, l, d), F32),
                   jax.ShapeDtypeStruct((b_, l, d), BF16),
                   jax.ShapeDtypeStruct((b_, l, rp), jnp.int32),
                   jax.ShapeDtypeStruct((b_, l, rp), F32),
                   jax.ShapeDtypeStruct((8, rp), F32)],
        scratch_shapes=[pltpu.VMEM((1, rp), F32)],
        compiler_params=_cparams(("arbitrary", "arbitrary")),
        name="merge_router",
    )(x, ga, gb, branch_a, branch_b, gate_m, w_out, norm_w.reshape(1, d), scale, shift,
      router_w3, router_b, tri)


def _expert_kernel(be_ref, nu_ref, x_ref, wg_ref, bg_ref, wu_ref, bu_ref, wd_ref, bd_ref, o_ref,
                   wg_bf, wu_bf, wd_bf):
    i = pl.program_id(0)
    new_expert = jnp.logical_or(i == 0, be_ref[i] != be_ref[jnp.maximum(i - 1, 0)])

    @pl.when(jnp.logical_and(new_expert, i < nu_ref[0]))
    def _():
        wg_bf[...] = wg_ref[0].astype(BF16)
        wu_bf[...] = wu_ref[0].astype(BF16)
        wd_bf[...] = wd_ref[0].astype(BF16)

    @pl.when(i < nu_ref[0])
    def _():
        xb = x_ref[...]
        gate = jnp.minimum(_dot(xb, wg_bf[...]) + bg_ref[0], SWIGLU_LIMIT)
        up = jnp.clip(_dot(xb, wu_bf[...]) + bu_ref[0], -SWIGLU_LIMIT, SWIGLU_LIMIT)
        act = (up + 1.0) * gate * _sigmoid(SWIGLU_ALPHA * gate)
        o_ref[...] = _dot(act.astype(BF16), wd_bf[...]) + bd_ref[0]

    @pl.when(i >= nu_ref[0])
    def _():
        o_ref[...] = jnp.zeros_like(o_ref)


def expert_blocks(block_expert, n_used, x_sorted, wg, bg, wu, bu, wd, bd):
    n_slots, d = x_sorted.shape
    de = wg.shape[2]
    nblk = n_slots // MOE_BLOCK
    wspec = lambda a, b: pl.BlockSpec((1, a, b), lambda i, be, nu: (be[i], 0, 0))
    grid_spec = pltpu.PrefetchScalarGridSpec(
        num_scalar_prefetch=2,
        grid=(nblk,),
        in_specs=[pl.BlockSpec((MOE_BLOCK, d), lambda i, be, nu: (i, 0)),
                  wspec(d, de), wspec(1, de), wspec(d, de), wspec(1, de), wspec(de, d), wspec(1, d)],
        out_specs=pl.BlockSpec((MOE_BLOCK, d), lambda i, be, nu: (i, 0)),
        scratch_shapes=[pltpu.VMEM((d, de), BF16), pltpu.VMEM((d, de), BF16), pltpu.VMEM((de, d), BF16)],
    )
    return pl.pallas_call(
        _expert_kernel,
        grid_spec=grid_spec,
        out_shape=jax.ShapeDtypeStruct((n_slots, d), F32),
        compiler_params=_cparams(("arbitrary",)),
        name="expert_blocks",
    )(block_expert, n_used, x_sorted, wg, bg, wu, bu, wd, bd)


def _final_kernel(h_ref, y_ref, rg_ref, g_ref, w_ref, o_ref):
    rg = rg_ref[0]
    ffn = y_ref[0, 0] * rg[:, 0:1]
    for k in range(1, TOP_K):
        ffn = ffn + y_ref[k, 0] * rg[:, k:k + 1]
    h = h_ref[0] + g_ref[0] * ffn
    o_ref[0] = _rms(h) * w_ref[...]


def final_combine_norm(h, y4, rg, gate_f, final_w):
    b_, l, d = h.shape
    tm = 512
    blk = lambda w=d: pl.BlockSpec((1, tm, w), lambda b, i: (b, i, 0))
    return pl.pallas_call(
        _final_kernel,
        grid=(b_, l // tm),
        in_specs=[blk(),
                  pl.BlockSpec((TOP_K, 1, tm, d), lambda b, i: (0, b, i, 0)),
                  blk(rg.shape[2]),
                  pl.BlockSpec((1, 1, d), lambda b, i: (b, 0, 0)),
                  pl.BlockSpec((1, d), lambda b, i: (0, 0))],
        out_specs=blk(),
        out_shape=jax.ShapeDtypeStruct((b_, l, d), F32),
        compiler_params=_cparams(("parallel", "arbitrary")),
        name="final_combine_norm",
    )(h, y4, rg, gate_f, final_w.reshape(1, d))


def _slot_tables(counts, n_pairs):
    padded = (counts + MOE_BLOCK - 1) // MOE_BLOCK * MOE_BLOCK
    padded_end = jnp.cumsum(padded)
    n_blocks = n_pairs // MOE_BLOCK + N_EXPERTS
    block_start = jnp.arange(n_blocks, dtype=jnp.int32) * MOE_BLOCK
    block_expert = jnp.minimum(jnp.sum(block_start[:, None] >= padded_end[None, :], axis=1),
                               N_EXPERTS - 1).astype(jnp.int32)
    n_used = (padded_end[-1] // MOE_BLOCK).astype(jnp.int32).reshape(1)
    return (padded_end - padded).astype(jnp.int32), block_expert, n_used, n_blocks * MOE_BLOCK


def kernel(x, c, ctx, c_ctx, ada_w, ada_b, norm_mix_w, w_in, s5_lam_re, s5_lam_im, s5_log_dt, s5_b_re, s5_b_im, s5_c_re, s5_c_im, s5_d, s5_glu_w, s5_glu_b, ssd_conv_w, ssd_conv_b, ssd_dt_bias, ssd_a_log, ssd_d, ssd_norm_w, w_branch_a, w_branch_b, w_out, norm_ffn_w, router_w, router_b, moe_w_gate, moe_b_gate, moe_w_up, moe_b_up, moe_w_down, moe_b_down, final_norm_w):
    assert ada_w.shape[0] == 1, "single-layer block"
    b_, l, d = x.shape
    s5_w = s5_d.shape[1]
    ssd_w = ssd_d.shape[1] * SSD_HEADDIM
    heads = ssd_d.shape[1]
    conv_dim = ssd_w + 2 * SSD_GROUPS * SSD_STATE
    c0, c1, c2, c3, c4 = (s5_w, s5_w + conv_dim, s5_w + conv_dim + 2 * heads,
                          s5_w + conv_dim + 2 * heads + ssd_w, s5_w + conv_dim + 2 * heads + ssd_w + d)

    cond = jnp.concatenate([c, c_ctx[None, :], jnp.zeros((7, d), F32)], axis=0)
    mod = ada_modulation(cond, ada_w[0], ada_b[0])
    mod6 = mod.reshape(mod.shape[0], 6, d)
    lat_mod = [mod6[:b_, k][:, None, :] for k in range(6)]
    ctx_mod = [jnp.broadcast_to(mod6[b_, k][None, None, :], (b_, 1, d)) for k in range(6)]
    sh_m, sc_m, g_m, sh_f, sc_f, g_f = lat_mod

    w = w_in[0]
    w_row = jnp.concatenate([w[:, :c0], w[:, c3:c4], w[:, c4:]], axis=1).astype(BF16)
    w_dt = jnp.zeros((d, DT_PAD), F32).at[:, :2 * heads].set(w[:, c1:c2])
    w_col = jnp.concatenate([w[:, c0:c1], w[:, c2:c3], w_dt], axis=1).astype(BF16)

    u_lat, ga, gb = in_projection(x, norm_mix_w[0], sc_m, sh_m, w_row, (s5_w, d, d),
                                  col_major=False, tokens_per_step=512)
    xbc_lat, z_lat, dtr_lat = in_projection(x, norm_mix_w[0], sc_m, sh_m, w_col, (conv_dim, ssd_w, DT_PAD),
                                            col_major=True, tokens_per_step=256)
    lc = ctx.shape[1]
    (u_ctx,) = in_projection(ctx, norm_mix_w[0], ctx_mod[1], ctx_mod[0], w_row[:, :s5_w], (s5_w,),
                             col_major=False, tokens_per_step=lc)
    w_col_ctx = jnp.concatenate([w_col[:, :conv_dim], w_col[:, conv_dim + ssd_w:]], axis=1)
    xbc_ctx, dtr_ctx = in_projection(ctx, norm_mix_w[0], ctx_mod[1], ctx_mod[0], w_col_ctx, (conv_dim, DT_PAD),
                                     col_major=False, tokens_per_step=lc)

    exp_f = _s5_expand(s5_lam_re[0, 0], s5_lam_im[0, 0], s5_log_dt[0, 0], s5_b_re[0, 0], s5_b_im[0, 0],
                       s5_c_re[0, 0], s5_c_im[0, 0], b_)
    exp_b = _s5_expand(s5_lam_re[0, 1], s5_lam_im[0, 1], s5_log_dt[0, 1], s5_b_re[0, 1], s5_b_im[0, 1],
                       s5_c_re[0, 1], s5_c_im[0, 1], b_)
    bexp, cexp, ar, ai = [jnp.stack([f, g]) for f, g in zip(exp_f, exp_b)]
    y_s5 = s5_scan(u_ctx, u_lat, bexp, cexp, ar, ai)
    branch_a = s5_glu_branch(y_s5, u_lat, s5_d[0], s5_glu_w[0].astype(BF16), s5_glu_b[0],
                             w_branch_a[0].astype(BF16))

    dt_bias_row = jnp.zeros((1, DT_PAD), F32).at[0, :2 * heads].set(ssd_dt_bias[0].reshape(-1))
    a_row = jnp.zeros((1, LANES), F32).at[0, :2 * heads].set(-jnp.exp(ssd_a_log[0].astype(F32)).reshape(-1))
    xc_lat, dt_lat = ssd_prep(xbc_lat, dtr_lat, ssd_conv_w[0], ssd_conv_b[0], dt_bias_row)
    xc_ctx, dt_ctx = ssd_prep(xbc_ctx, dtr_ctx, ssd_conv_w[0], ssd_conv_b[0], dt_bias_row)
    s_f = ssd_scan(xc_ctx, dt_ctx, a_row, None, reverse=False, width=ssd_w)
    s_b = ssd_scan(xc_ctx, dt_ctx, a_row, None, reverse=True, width=ssd_w)
    y_f = ssd_scan(xc_lat, dt_lat, a_row, s_f, reverse=False, width=ssd_w)
    y_b = ssd_scan(xc_lat, dt_lat, a_row, s_b, reverse=True, width=ssd_w)
    d_row = jnp.repeat(ssd_d[0].astype(F32), SSD_HEADDIM).reshape(1, ssd_w)
    branch_b = ssd_out_branch(y_f, y_b, xc_lat, z_lat, d_row, ssd_norm_w[0], w_branch_b[0].astype(BF16))

    branch_b_rm = branch_b.reshape(b_, GRID_W, l // GRID_W, d).transpose(0, 2, 1, 3).reshape(b_, l, d)

    rw = jnp.zeros((d, ROUTER_PAD), F32).at[:, :N_EXPERTS].set(router_w[0])
    rw3 = jnp.stack(_split3(rw))
    rb = jnp.zeros((1, ROUTER_PAD), F32).at[0, :N_EXPERTS].set(router_b[0])
    h1, xn, ri, rg, cnt = merge_router(x, ga, gb, branch_a, branch_b_rm, g_m, w_out[0].astype(BF16),
                                       norm_ffn_w[0], sc_f, sh_f, rw3, rb)
    n = b_ * l
    counts = cnt[0, :N_EXPERTS].astype(jnp.int32)
    slot_start, block_expert, n_used, n_slots = _slot_tables(counts, n * TOP_K)
    ri = ri.reshape(n, ROUTER_PAD)
    pair_slot = slot_start[ri[:, :TOP_K]] + ri[:, TOP_K:2 * TOP_K]
    slot_token = jnp.full((n_slots,), n, jnp.int32).at[pair_slot.reshape(-1)].set(
        jnp.arange(n * TOP_K, dtype=jnp.int32) // TOP_K)
    x_sorted = jnp.take(xn.reshape(n, d), slot_token, axis=0, mode='fill', fill_value=0)
    y_sorted = expert_blocks(block_expert, n_used, x_sorted,
                             moe_w_gate[0], moe_b_gate[0][:, None, :],
                             moe_w_up[0], moe_b_up[0][:, None, :],
                             moe_w_down[0], moe_b_down[0][:, None, :])
    y4 = y_sorted[pair_slot.T].reshape(TOP_K, b_, l, d)

    return final_combine_norm(h1, y4, rg, g_f, final_norm_w)
```

```python
import functools
import math

import jax
import jax.numpy as jnp
from jax import lax
from jax.experimental import pallas as pl
from jax.experimental.pallas import tpu as pltpu

F32 = jnp.float32
BF16 = jnp.bfloat16

EPS = 1e-6
GRID_W = 64
LANES = 128

S5_GROUP = 16
S5_STATE = 64
S5_COLS = 256
S5_CHUNK = 128

SSD_HEADDIM = 64
SSD_GROUPS = 8
SSD_STATE = 128
SSD_CHUNK = 128
DT_PAD = 256
MASKED_EXPONENT = -1e30

N_EXPERTS = 32
TOP_K = 4
SWIGLU_ALPHA = 1.702
SWIGLU_LIMIT = 7.0
MOE_BLOCK = 512
ROUTER_PAD = 128

VMEM_LIMIT_BYTES = 56 * 1024 * 1024


def _cparams(sem):
    return pltpu.CompilerParams(dimension_semantics=sem, vmem_limit_bytes=VMEM_LIMIT_BYTES)


def _dot(a, b):
    return jnp.dot(a, b, preferred_element_type=F32)


def _split3(x):
    hi = x.astype(BF16)
    r = x - hi.astype(F32)
    mid = r.astype(BF16)
    lo = (r - mid.astype(F32)).astype(BF16)
    return hi, mid, lo


def _sigmoid(x):
    return 0.5 * (1.0 + jnp.tanh(0.5 * x))


def _silu(x):
    return x * _sigmoid(x)


def _softplus(x):
    return jnp.maximum(x, 0.0) + jnp.log1p(jnp.exp(-jnp.abs(x)))


def _gelu_tanh(x):
    c = math.sqrt(2.0 / math.pi)
    return 0.5 * x * (1.0 + jnp.tanh(c * (x + 0.044715 * (x * x * x))))


def _rms(x):
    return x * lax.rsqrt(jnp.mean(x * x, axis=-1, keepdims=True) + EPS)


def _ada_kernel(c_ref, w_ref, b_ref, o_ref):
    s = _silu(c_ref[...]).astype(BF16)
    o_ref[...] = _dot(s, w_ref[...].astype(BF16)) + b_ref[...]


def ada_modulation(cond, w, b):
    n, d = cond.shape
    cols = w.shape[1]
    tn = 1024
    return pl.pallas_call(
        _ada_kernel,
        grid=(cols // tn,),
        in_specs=[pl.BlockSpec((n, d), lambda j: (0, 0)),
                  pl.BlockSpec((d, tn), lambda j: (0, j)),
                  pl.BlockSpec((1, tn), lambda j: (0, j))],
        out_specs=pl.BlockSpec((n, tn), lambda j: (0, j)),
        out_shape=jax.ShapeDtypeStruct((n, cols), F32),
        compiler_params=_cparams(("arbitrary",)),
        name="ada_modulation",
    )(cond, w, b.reshape(1, cols))


def _inproj_kernel(x_ref, nw_ref, sc_ref, sh_ref, w_ref, *rest, d_model, widths):
    o_refs = rest[:len(widths)]
    u_ref = rest[len(widths)]
    r = x_ref.shape[1]
    k = x_ref.shape[2] // d_model
    nw = nw_ref[...]
    sc = 1.0 + sc_ref[0]
    sh = sh_ref[0]
    for c in range(k):
        xs = x_ref[0, :, c * d_model:(c + 1) * d_model]
        u_ref[c * r:(c + 1) * r, :] = ((_rms(xs) * nw) * sc + sh).astype(BF16)
    off = 0
    for o_ref, wd in zip(o_refs, widths):
        step = min(wd, 512)
        for n0 in range(0, wd, step):
            o_ref[0, :, n0:n0 + step] = _dot(u_ref[...], w_ref[:, off + n0:off + n0 + step]).astype(o_ref.dtype)
        off += wd


def in_projection(x, norm_w, scale, shift, w, widths, *, col_major, tokens_per_step):
    b_, l, d = x.shape
    if col_major:
        rows = l // GRID_W
        k = tokens_per_step // rows
        xv = x.reshape(b_, rows, GRID_W * d)
        x_spec = pl.BlockSpec((1, rows, k * d), lambda b, i: (b, 0, i))
        steps = GRID_W // k
    else:
        xv = x
        x_spec = pl.BlockSpec((1, tokens_per_step, d), lambda b, i: (b, i, 0))
        steps = l // tokens_per_step
    vec = lambda: pl.BlockSpec((1, 1, d), lambda b, i: (b, 0, 0))
    return pl.pallas_call(
        functools.partial(_inproj_kernel, d_model=d, widths=tuple(widths)),
        grid=(b_, steps),
        in_specs=[x_spec,
                  pl.BlockSpec((1, d), lambda b, i: (0, 0)),
                  vec(), vec(),
                  pl.BlockSpec(w.shape, lambda b, i: (0, 0), pipeline_mode=pl.Buffered(1))],
        out_specs=[pl.BlockSpec((1, tokens_per_step, wd), lambda b, i: (b, i, 0)) for wd in widths],
        out_shape=[jax.ShapeDtypeStruct((b_, l, wd), F32) for wd in widths],
        scratch_shapes=[pltpu.VMEM((tokens_per_step, d), BF16)],
        compiler_params=_cparams(("parallel", "arbitrary")),
        name="in_projection_col" if col_major else "in_projection_row",
    )(xv, norm_w.reshape(1, d), scale, shift, w)


def _s5_kernel(uc_ref, ul_ref, bexp_ref, cexp_ref, ar_ref, ai_ref, y_ref, *scratch, n_ctx):
    nslab = (len(scratch) - 2) // 2
    ubufs, ybufs = scratch[:nslab], scratch[nslab:2 * nslab]
    bu, state = scratch[2 * nslab:]
    d = pl.program_id(0)
    i = pl.program_id(2)
    nb, tl = ul_ref.shape[0], ul_ref.shape[1]
    half = state.shape[1] // 2
    is_ctx = i < n_ctx

    @pl.when(i == 0)
    def _():
        state[...] = jnp.zeros_like(state)

    for b in range(nb):
        ub = jnp.where(is_ctx, uc_ref[b], ul_ref[b])
        for s, ubuf in enumerate(ubufs):
            ubuf[pl.ds(b, tl, stride=nb), :] = ub[:, s * LANES:(s + 1) * LANES]
    u_tm = jnp.concatenate([ubuf[...] for ubuf in ubufs], axis=1)
    bu[...] = _dot(u_tm.astype(BF16), bexp_ref[0, 0])

    ar = ar_ref[0, 0]
    ai = ai_ref[0, 0]

    def step(t, carry):
        xr, xi = carry
        tok = jnp.where(d == 0, t, tl - 1 - t)
        row = pl.multiple_of(tok * nb, nb)
        br = bu[pl.ds(row, nb), 0:half]
        bi = bu[pl.ds(row, nb), half:2 * half]
        nxr = ar * xr - ai * xi + br
        nxi = ar * xi + ai * xr + bi
        bu[pl.ds(row, nb), 0:half] = nxr
        bu[pl.ds(row, nb), half:2 * half] = nxi
        return nxr, nxi

    xr, xi = lax.fori_loop(0, tl, step, (state[:, 0:half], state[:, half:2 * half]), unroll=2)
    state[:, 0:half] = xr
    state[:, half:2 * half] = xi

    @pl.when(jnp.logical_not(is_ctx))
    def _():
        y_tm = _dot(bu[...].astype(BF16), cexp_ref[0, 0])
        for s, ybuf in enumerate(ybufs):
            ybuf[...] = y_tm[:, s * LANES:(s + 1) * LANES]
        for b in range(nb):
            for s, ybuf in enumerate(ybufs):
                y_ref[0, b, :, s * LANES:(s + 1) * LANES] = ybuf[pl.ds(b, tl, stride=nb), :]


def s5_scan(u_ctx, u_lat, bexp, cexp, ar, ai):
    b_, l_lat, w = u_lat.shape
    l_ctx = u_ctx.shape[1]
    tl = S5_CHUNK
    n_ctx, n_lat = l_ctx // tl, l_lat // tl
    ncb = w // S5_COLS
    ns = bexp.shape[-1]

    def ctx_idx(d, i):
        return jnp.where(d == 0, jnp.minimum(i, n_ctx - 1), jnp.maximum(n_ctx - 1 - i, 0))

    def lat_idx(d, i):
        k = jnp.maximum(i - n_ctx, 0)
        return jnp.where(d == 0, k, n_lat - 1 - k)

    return pl.pallas_call(
        functools.partial(_s5_kernel, n_ctx=n_ctx),
        grid=(2, ncb, n_ctx + n_lat),
        in_specs=[pl.BlockSpec((b_, tl, S5_COLS), lambda d, j, i: (0, ctx_idx(d, i), j)),
                  pl.BlockSpec((b_, tl, S5_COLS), lambda d, j, i: (0, lat_idx(d, i), j)),
                  pl.BlockSpec((1, 1, S5_COLS, ns), lambda d, j, i: (d, j, 0, 0)),
                  pl.BlockSpec((1, 1, ns, S5_COLS), lambda d, j, i: (d, j, 0, 0)),
                  pl.BlockSpec((1, 1, b_, ns // 2), lambda d, j, i: (d, j, 0, 0)),
                  pl.BlockSpec((1, 1, b_, ns // 2), lambda d, j, i: (d, j, 0, 0))],
        out_specs=pl.BlockSpec((1, b_, tl, S5_COLS), lambda d, j, i: (d, 0, lat_idx(d, i), j)),
        out_shape=jax.ShapeDtypeStruct((2, b_, l_lat, w), F32),
        scratch_shapes=([pltpu.VMEM((tl * b_, LANES), F32)] * (2 * (S5_COLS // LANES))
                        + [pltpu.VMEM((tl * b_, ns), F32), pltpu.VMEM((b_, ns), F32)]),
        compiler_params=_cparams(("parallel", "parallel", "arbitrary")),
        name="s5_scan",
    )(u_ctx, u_lat, bexp, cexp, ar, ai)


def _s5_expand(lam_re, lam_im, log_dt, b_re, b_im, c_re, c_im, batch):
    lam = lax.complex(lam_re.astype(F32), lam_im.astype(F32))
    step = jnp.exp(log_dt.astype(F32))[:, None]
    a_bar = jnp.exp(lam * step)
    b_mat = lax.complex(b_re.astype(F32), b_im.astype(F32))
    b_bar = ((a_bar - 1.0) / lam)[:, :, None] * b_mat
    g, p, h = b_bar.shape
    gl = S5_COLS // h
    ncb = g // gl
    eye = jnp.eye(gl, dtype=F32)

    def expand_b(t):
        return jnp.einsum('jgph,gk->jghkp', t.reshape(ncb, gl, p, h), eye).reshape(ncb, gl * h, gl * p)

    def expand_c(t):
        return jnp.einsum('jghp,gk->jgpkh', t.reshape(ncb, gl, h, p), eye).reshape(ncb, gl * p, gl * h)

    bexp = jnp.concatenate([expand_b(jnp.real(b_bar)), expand_b(jnp.imag(b_bar))], axis=-1)
    cexp = jnp.concatenate([expand_c(c_re.astype(F32)), -expand_c(c_im.astype(F32))], axis=1)
    ar = jnp.broadcast_to(jnp.real(a_bar).reshape(ncb, 1, gl * p), (ncb, batch, gl * p))
    ai = jnp.broadcast_to(jnp.imag(a_bar).reshape(ncb, 1, gl * p), (ncb, batch, gl * p))
    return bexp.astype(BF16), cexp.astype(BF16), ar, ai


def _s5_glu_kernel(yf_ref, yb_ref, u_ref, d_ref, gw_ref, gb_ref, wa_ref, o_ref):
    y = yf_ref[0, 0] + yb_ref[0, 0] + d_ref[...] * u_ref[0]
    g = _gelu_tanh(y)
    gate = _sigmoid(_dot(g.astype(BF16), gw_ref[...]) + gb_ref[...])
    o_ref[0] = _dot((g * gate).astype(BF16), wa_ref[...])


def s5_glu_branch(y2, u, s5_d, glu_w, glu_b, w_branch):
    b_, l, w = u.shape
    dm = w_branch.shape[1]
    tm = 256
    row = lambda: pl.BlockSpec((1, w), lambda b, i: (0, 0))
    return pl.pallas_call(
        _s5_glu_kernel,
        grid=(b_, l // tm),
        in_specs=[pl.BlockSpec((1, 1, tm, w), lambda b, i: (0, b, i, 0)),
                  pl.BlockSpec((1, 1, tm, w), lambda b, i: (1, b, i, 0)),
                  pl.BlockSpec((1, tm, w), lambda b, i: (b, i, 0)),
                  row(),
                  pl.BlockSpec((w, w), lambda b, i: (0, 0)),
                  row(),
                  pl.BlockSpec((w, dm), lambda b, i: (0, 0))],
        out_specs=pl.BlockSpec((1, tm, dm), lambda b, i: (b, i, 0)),
        out_shape=jax.ShapeDtypeStruct((b_, l, dm), F32),
        compiler_params=_cparams(("parallel", "arbitrary")),
        name="s5_glu_branch",
    )(y2, y2, u, s5_d.reshape(1, w), glu_w, glu_b.reshape(1, w), w_branch)


def _ssd_prep_kernel(prev_ref, cur_ref, next_ref, dt_ref, cw_ref, cb_ref, dtb_ref, xo_ref, dto_ref, *, kw):
    i = pl.program_id(1)
    last = pl.num_programs(1) - 1
    tl = cur_ref.shape[1]
    halo = prev_ref.shape[1]
    pad = kw // 2
    prev = jnp.where(i == 0, 0.0, prev_ref[0])
    nxt = jnp.where(i == last, 0.0, next_ref[0])
    ext = jnp.concatenate([prev, cur_ref[0], nxt], axis=0)
    acc = cb_ref[...] + jnp.zeros((tl, cur_ref.shape[2]), F32)
    rows = ext.shape[0]
    for k in range(kw):
        shifted = ext if k == pad else pltpu.roll(ext, (pad - k) % rows, 0)
        acc = acc + shifted[halo:halo + tl, :] * cw_ref[k:k + 1, :]
    xo_ref[0] = _silu(acc)
    dto_ref[0] = _softplus(dt_ref[0] + dtb_ref[...])


def ssd_prep(xbc, dt_raw, conv_w, conv_b, dt_bias_row):
    b_, l, c = xbc.shape
    tl = min(256, l)
    halo = 8
    hb = tl // halo
    nh = l // halo
    kw = conv_w.shape[0]
    cw = jnp.zeros((8, c), F32).at[:kw].set(conv_w)
    dp = dt_raw.shape[2]
    return pl.pallas_call(
        functools.partial(_ssd_prep_kernel, kw=kw),
        grid=(b_, l // tl),
        in_specs=[pl.BlockSpec((1, halo, c), lambda b, i: (b, jnp.maximum(i * hb - 1, 0), 0)),
                  pl.BlockSpec((1, tl, c), lambda b, i: (b, i, 0)),
                  pl.BlockSpec((1, halo, c), lambda b, i: (b, jnp.minimum((i + 1) * hb, nh - 1), 0)),
                  pl.BlockSpec((1, tl, dp), lambda b, i: (b, i, 0)),
                  pl.BlockSpec((8, c), lambda b, i: (0, 0)),
                  pl.BlockSpec((1, c), lambda b, i: (0, 0)),
                  pl.BlockSpec((1, dp), lambda b, i: (0, 0))],
        out_specs=[pl.BlockSpec((1, tl, c), lambda b, i: (b, i, 0)),
                   pl.BlockSpec((1, tl, dp), lambda b, i: (b, i, 0))],
        out_shape=[jax.ShapeDtypeStruct((b_, l, c), F32),
                   jax.ShapeDtypeStruct((b_, l, dp), F32)],
        compiler_params=_cparams(("parallel", "arbitrary")),
        name="ssd_prep",
    )(xbc, xbc, xbc, dt_raw, cw, conv_b.reshape(1, c), dt_bias_row)


def _ssd_scan_kernel(*refs, reverse, width, col0, with_output):
    if with_output:
        x_ref, dt_ref, a_ref, e_ref, s0_ref, y_ref, state, bt_scr = refs
    else:
        x_ref, dt_ref, a_ref, e_ref, so_ref, state, bt_scr = refs
    i = pl.program_id(1)
    q = x_ref.shape[1]
    n = SSD_STATE
    p = SSD_HEADDIM
    ng = SSD_GROUPS
    gw = width // ng
    hpg = gw // p

    @pl.when(i == 0)
    def _():
        state[...] = s0_ref[0] if with_output else jnp.zeros_like(state)

    dt = dt_ref[0][:, :LANES]
    dta = dt * a_ref[...]
    ri = lax.broadcasted_iota(jnp.int32, (q, q), 0)
    ci = lax.broadcasted_iota(jnp.int32, (q, q), 1)
    mask = (ri <= ci) if reverse else (ri >= ci)
    tri = mask.astype(BF16)
    hi, mid, lo = _split3(dta)
    acum = _dot(tri, hi) + _dot(tri, mid) + _dot(tri, lo)

    def expand(v):
        h_, m_, l_ = _split3(v)
        return _dot(jnp.concatenate([h_, m_], axis=1), e_ref[...]) + _dot(l_, e_ref[0:LANES, :])

    acum_e = expand(acum)
    dt_e = expand(dt)
    last = 0 if reverse else q - 1
    tot_e = acum_e[last:last + 1, :]
    xdt = x_ref[0, :, 0:width] * dt_e
    xw = (xdt * jnp.exp(tot_e - acum_e)).astype(BF16)
    dec_e = jnp.exp(tot_e)
    for g in range(ng):
        bt_scr[g] = x_ref[0, :, width + g * n: width + (g + 1) * n].T.astype(BF16)

    if with_output:
        xdt_bf = xdt.astype(BF16)
        eac = jnp.exp(acum_e)
        acum_t = acum.T
        head_of_lane = lax.broadcasted_iota(jnp.int32, (1, gw), 1) // p
        for g in range(ng):
            sl = slice(g * gw, (g + 1) * gw)
            cmat = x_ref[0, :, width + (ng + g) * n: width + (ng + g + 1) * n].astype(BF16)
            cb = _dot(cmat, bt_scr[g])
            yoff = _dot(cmat, state[g].astype(BF16)) * eac[:, sl]
            xg = xdt_bf[:, sl]
            gms, rhs = [], []
            for r in range(hpg):
                hc = col0 + g * hpg + r
                diff = acum[:, hc:hc + 1] - acum_t[hc:hc + 1, :]
                gms.append((cb * jnp.exp(jnp.where(mask, diff, MASKED_EXPONENT))).astype(BF16))
                rhs.append(jnp.where(head_of_lane == r, xg, jnp.zeros_like(xg)))
            yd = _dot(jnp.concatenate(gms, axis=1), jnp.concatenate(rhs, axis=0))
            y_ref[0, :, sl] = yd + yoff

    for g in range(ng):
        sl = slice(g * gw, (g + 1) * gw)
        state[g] = state[g] * dec_e[:, sl] + _dot(bt_scr[g], xw[:, sl])
    if not with_output:
        so_ref[0] = state[...]


def ssd_scan(xc, dt, a_row, init_state, *, reverse, width):
    b_, l, c = xc.shape
    q = SSD_CHUNK
    nc = l // q
    dp = dt.shape[2]
    heads = width // SSD_HEADDIM
    col0 = heads if reverse else 0
    with_output = init_state is not None
    gw = width // SSD_GROUPS
    col = lax.broadcasted_iota(jnp.int32, (2 * LANES, width), 0) % LANES
    expand_mat = (col == col0 + lax.broadcasted_iota(jnp.int32, (2 * LANES, width), 1) // SSD_HEADDIM).astype(BF16)
    chunk = (lambda i: nc - 1 - i) if reverse else (lambda i: i)
    state_spec = pl.BlockSpec((1, SSD_GROUPS, SSD_STATE, gw), lambda b, i: (b, 0, 0, 0))
    in_specs = [pl.BlockSpec((1, q, c), lambda b, i: (b, chunk(i), 0)),
                pl.BlockSpec((1, q, dp), lambda b, i: (b, chunk(i), 0)),
                pl.BlockSpec((1, LANES), lambda b, i: (0, 0)),
                pl.BlockSpec((2 * LANES, width), lambda b, i: (0, 0))]
    args = [xc, dt, a_row, expand_mat]
    if with_output:
        in_specs.append(state_spec)
        args.append(init_state)
        out_specs = pl.BlockSpec((1, q, width), lambda b, i: (b, chunk(i), 0))
        out_shape = jax.ShapeDtypeStruct((b_, l, width), F32)
    else:
        out_specs = state_spec
        out_shape = jax.ShapeDtypeStruct((b_, SSD_GROUPS, SSD_STATE, gw), F32)
    return pl.pallas_call(
        functools.partial(_ssd_scan_kernel, reverse=reverse, width=width, col0=col0, with_output=with_output),
        grid=(b_, nc),
        in_specs=in_specs,
        out_specs=out_specs,
        out_shape=out_shape,
        scratch_shapes=[pltpu.VMEM((SSD_GROUPS, SSD_STATE, gw), F32),
                        pltpu.VMEM((SSD_GROUPS, SSD_STATE, q), BF16)],
        compiler_params=_cparams(("parallel", "arbitrary")),
        name=("ssd_scan" if with_output else "ssd_state") + ("_bwd" if reverse else "_fwd"),
    )(*args)


def _ssd_out_kernel(yf_ref, yb_ref, x_ref, z_ref, d_ref, nw_ref, wb_ref, o_ref):
    y = (yf_ref[0] + yb_ref[0] + d_ref[...] * x_ref[0]) * _silu(z_ref[0])
    w = y.shape[1]
    gw = w // SSD_GROUPS
    parts = [_rms(y[:, g * gw:(g + 1) * gw]) for g in range(SSD_GROUPS)]
    yn = (jnp.concatenate(parts, axis=1) * nw_ref[...]).astype(BF16)
    o_ref[0] = _dot(yn, wb_ref[...])


def ssd_out_branch(yf, yb, xc_lat, z, d_row, norm_w, w_branch):
    b_, l, w = yf.shape
    dm = w_branch.shape[1]
    tm = 256
    blk = lambda: pl.BlockSpec((1, tm, w), lambda b, i: (b, i, 0))
    row = lambda: pl.BlockSpec((1, w), lambda b, i: (0, 0))
    return pl.pallas_call(
        _ssd_out_kernel,
        grid=(b_, l // tm),
        in_specs=[blk(), blk(), blk(), blk(), row(), row(),
                  pl.BlockSpec((w, dm), lambda b, i: (0, 0))],
        out_specs=pl.BlockSpec((1, tm, dm), lambda b, i: (b, i, 0)),
        out_shape=jax.ShapeDtypeStruct((b_, l, dm), F32),
        compiler_params=_cparams(("parallel", "arbitrary")),
        name="ssd_out_branch",
    )(yf, yb, xc_lat, z, d_row, norm_w.reshape(1, w), w_branch)


def _merge_router_kernel(x_ref, ga_ref, gb_ref, ba_ref, bb_ref, gm_ref, wo_ref, nw_ref, sc_ref, sh_ref,
                         rw_ref, rb_ref, tri_ref, h_ref, xn_ref, ri_ref, rg_ref, cnt_ref, carry):
    @pl.when(jnp.logical_and(pl.program_id(0) == 0, pl.program_id(1) == 0))
    def _():
        carry[...] = jnp.zeros_like(carry)

    m = _sigmoid(ga_ref[0]) * ba_ref[0] + _sigmoid(gb_ref[0]) * bb_ref[0]
    h = x_ref[0] + gm_ref[0] * _dot(m.astype(BF16), wo_ref[...])
    h_ref[0] = h
    u = (_rms(h) * nw_ref[...]) * (1.0 + sc_ref[0]) + sh_ref[0]
    xn_ref[0] = u.astype(BF16)
    uh, um, ul = _split3(u)
    wh, wm, wl = rw_ref[0], rw_ref[1], rw_ref[2]
    lg = (_dot(uh, wh) + (_dot(uh, wm) + _dot(um, wh))
          + (_dot(uh, wl) + _dot(um, wm) + _dot(ul, wh))) + rb_ref[...]

    tm, rp = lg.shape
    lane = lax.broadcasted_iota(jnp.int32, (tm, rp), 1)
    lane_f = lane.astype(F32)
    neg = jnp.float32(-jnp.inf)
    cur = jnp.where(lane < N_EXPERTS, lg, neg)
    onehots, vals, idxs = [], [], []
    for _ in range(TOP_K):
        mx = jnp.max(cur, axis=-1, keepdims=True)
        idx = jnp.min(jnp.where(cur == mx, lane_f, float(rp)), axis=-1, keepdims=True)
        oh = lane_f == idx
        onehots.append(oh)
        vals.append(mx)
        idxs.append(idx)
        cur = jnp.where(oh, neg, cur)
    exps = [jnp.exp(v - vals[0]) for v in vals]
    denom = exps[0] + exps[1] + exps[2] + exps[3]
    oh_all = (onehots[0] | onehots[1] | onehots[2] | onehots[3]).astype(F32)
    before = _dot(tri_ref[...], oh_all.astype(BF16)) + carry[...]
    ri = jnp.zeros((tm, rp), jnp.int32)
    rg = jnp.zeros((tm, rp), F32)
    for k in range(TOP_K):
        rank = jnp.sum(jnp.where(onehots[k], before, 0.0), axis=-1, keepdims=True).astype(jnp.int32)
        ri = jnp.where(lane == k, idxs[k].astype(jnp.int32), ri)
        ri = jnp.where(lane == TOP_K + k, rank, ri)
        rg = jnp.where(lane == k, exps[k] / denom, rg)
    ri_ref[0] = ri
    rg_ref[0] = rg
    carry[...] = carry[...] + jnp.sum(oh_all, axis=0, keepdims=True)
    cnt_ref[...] = jnp.broadcast_to(carry[...], cnt_ref.shape)


def merge_router(x, ga, gb, branch_a, branch_b, gate_m, w_out, norm_w, scale, shift, router_w3, router_b):
    b_, l, d = x.shape
    tm = 512
    rp = router_w3.shape[2]
    tri = (lax.broadcasted_iota(jnp.int32, (tm, tm), 0) > lax.broadcasted_iota(jnp.int32, (tm, tm), 1)).astype(BF16)
    blk = lambda w=d: pl.BlockSpec((1, tm, w), lambda b, i: (b, i, 0))
    vec = lambda: pl.BlockSpec((1, 1, d), lambda b, i: (b, 0, 0))
    return pl.pallas_call(
        _merge_router_kernel,
        grid=(b_, l // tm),
        in_specs=[blk(), blk(), blk(), blk(), blk(), vec(),
                  pl.BlockSpec((d, d), lambda b, i: (0, 0)),
                  pl.BlockSpec((1, d), lambda b, i: (0, 0)),
                  vec(), vec(),
                  pl.BlockSpec((3, d, rp), lambda b, i: (0, 0, 0)),
                  pl.BlockSpec((1, rp), lambda b, i: (0, 0)),
                  pl.BlockSpec((tm, tm), lambda b, i: (0, 0))],
        out_specs=[blk(), blk(), blk(rp), blk(rp),
                   pl.BlockSpec((8, rp), lambda b, i: (0, 0))],
        out_shape=[jax.ShapeDtypeStruct((b_, l, d), F32),
                   jax.ShapeDtypeStruct((b_, l, d), BF16),
                   jax.ShapeDtypeStruct((b_, l, rp), jnp.int32),
                   jax.ShapeDtypeStruct((b_, l, rp), F32),
                   jax.ShapeDtypeStruct((8, rp), F32)],
        scratch_shapes=[pltpu.VMEM((1, rp), F32)],
        compiler_params=_cparams(("arbitrary", "arbitrary")),
        name="merge_router",
    )(x, ga, gb, branch_a, branch_b, gate_m, w_out, norm_w.reshape(1, d), scale, shift,
      router_w3, router_b, tri)


def _expert_kernel(be_ref, nu_ref, x_ref, wg_ref, bg_ref, wu_ref, bu_ref, wd_ref, bd_ref, o_ref,
                   wg_bf, wu_bf, wd_bf):
    i = pl.program_id(0)
    new_expert = jnp.logical_or(i == 0, be_ref[i] != be_ref[jnp.maximum(i - 1, 0)])

    @pl.when(jnp.logical_and(new_expert, i < nu_ref[0]))
    def _():
        wg_bf[...] = wg_ref[0].astype(BF16)
        wu_bf[...] = wu_ref[0].astype(BF16)
        wd_bf[...] = wd_ref[0].astype(BF16)

    @pl.when(i < nu_ref[0])
    def _():
        xb = x_ref[...]
        gate = jnp.minimum(_dot(xb, wg_bf[...]) + bg_ref[0], SWIGLU_LIMIT)
        up = jnp.clip(_dot(xb, wu_bf[...]) + bu_ref[0], -SWIGLU_LIMIT, SWIGLU_LIMIT)
        act = (up + 1.0) * gate * _sigmoid(SWIGLU_ALPHA * gate)
        o_ref[...] = (_dot(act.astype(BF16), wd_bf[...]) + bd_ref[0]).astype(o_ref.dtype)

    @pl.when(i >= nu_ref[0])
    def _():
        o_ref[...] = jnp.zeros_like(o_ref)


def expert_blocks(block_expert, n_used, x_sorted, wg, bg, wu, bu, wd, bd):
    n_slots, d = x_sorted.shape
    de = wg.shape[2]
    nblk = n_slots // MOE_BLOCK
    wspec = lambda a, b: pl.BlockSpec((1, a, b), lambda i, be, nu: (be[i], 0, 0))
    grid_spec = pltpu.PrefetchScalarGridSpec(
        num_scalar_prefetch=2,
        grid=(nblk,),
        in_specs=[pl.BlockSpec((MOE_BLOCK, d), lambda i, be, nu: (i, 0)),
                  wspec(d, de), wspec(1, de), wspec(d, de), wspec(1, de), wspec(de, d), wspec(1, d)],
        out_specs=pl.BlockSpec((MOE_BLOCK, d), lambda i, be, nu: (i, 0)),
        scratch_shapes=[pltpu.VMEM((d, de), BF16), pltpu.VMEM((d, de), BF16), pltpu.VMEM((de, d), BF16)],
    )
    return pl.pallas_call(
        _expert_kernel,
        grid_spec=grid_spec,
        out_shape=jax.ShapeDtypeStruct((n_slots, d), BF16),
        compiler_params=_cparams(("arbitrary",)),
        name="expert_blocks",
    )(block_expert, n_used, x_sorted, wg, bg, wu, bu, wd, bd)


def _final_kernel(h_ref, y_ref, rg_ref, g_ref, w_ref, o_ref):
    rg = rg_ref[0]
    ffn = y_ref[0, 0].astype(F32) * rg[:, 0:1]
    for k in range(1, TOP_K):
        ffn = ffn + y_ref[k, 0].astype(F32) * rg[:, k:k + 1]
    h = h_ref[0] + g_ref[0] * ffn
    o_ref[0] = _rms(h) * w_ref[...]


def final_combine_norm(h, y4, rg, gate_f, final_w):
    b_, l, d = h.shape
    tm = 512
    blk = lambda w=d: pl.BlockSpec((1, tm, w), lambda b, i: (b, i, 0))
    return pl.pallas_call(
        _final_kernel,
        grid=(b_, l // tm),
        in_specs=[blk(),
                  pl.BlockSpec((TOP_K, 1, tm, d), lambda b, i: (0, b, i, 0)),
                  blk(rg.shape[2]),
                  pl.BlockSpec((1, 1, d), lambda b, i: (b, 0, 0)),
                  pl.BlockSpec((1, d), lambda b, i: (0, 0))],
        out_specs=blk(),
        out_shape=jax.ShapeDtypeStruct((b_, l, d), F32),
        compiler_params=_cparams(("parallel", "arbitrary")),
        name="final_combine_norm",
    )(h, y4, rg, gate_f, final_w.reshape(1, d))


def _slot_tables(counts, n_pairs):
    padded = (counts + MOE_BLOCK - 1) // MOE_BLOCK * MOE_BLOCK
    padded_end = jnp.cumsum(padded)
    n_blocks = n_pairs // MOE_BLOCK + N_EXPERTS
    block_start = jnp.arange(n_blocks, dtype=jnp.int32) * MOE_BLOCK
    block_expert = jnp.minimum(jnp.sum(block_start[:, None] >= padded_end[None, :], axis=1),
                               N_EXPERTS - 1).astype(jnp.int32)
    n_used = (padded_end[-1] // MOE_BLOCK).astype(jnp.int32).reshape(1)
    return (padded_end - padded).astype(jnp.int32), block_expert, n_used, n_blocks * MOE_BLOCK


def kernel(x, c, ctx, c_ctx, ada_w, ada_b, norm_mix_w, w_in, s5_lam_re, s5_lam_im, s5_log_dt, s5_b_re, s5_b_im, s5_c_re, s5_c_im, s5_d, s5_glu_w, s5_glu_b, ssd_conv_w, ssd_conv_b, ssd_dt_bias, ssd_a_log, ssd_d, ssd_norm_w, w_branch_a, w_branch_b, w_out, norm_ffn_w, router_w, router_b, moe_w_gate, moe_b_gate, moe_w_up, moe_b_up, moe_w_down, moe_b_down, final_norm_w):
    assert ada_w.shape[0] == 1, "single-layer block"
    b_, l, d = x.shape
    s5_w = s5_d.shape[1]
    ssd_w = ssd_d.shape[1] * SSD_HEADDIM
    heads = ssd_d.shape[1]
    conv_dim = ssd_w + 2 * SSD_GROUPS * SSD_STATE
    c0, c1, c2, c3, c4 = (s5_w, s5_w + conv_dim, s5_w + conv_dim + 2 * heads,
                          s5_w + conv_dim + 2 * heads + ssd_w, s5_w + conv_dim + 2 * heads + ssd_w + d)

    cond = jnp.concatenate([c, c_ctx[None, :], jnp.zeros((7, d), F32)], axis=0)
    mod = ada_modulation(cond, ada_w[0], ada_b[0])
    mod6 = mod.reshape(mod.shape[0], 6, d)
    lat_mod = [mod6[:b_, k][:, None, :] for k in range(6)]
    ctx_mod = [jnp.broadcast_to(mod6[b_, k][None, None, :], (b_, 1, d)) for k in range(6)]
    sh_m, sc_m, g_m, sh_f, sc_f, g_f = lat_mod

    w = w_in[0]
    w_row = jnp.concatenate([w[:, :c0], w[:, c3:c4], w[:, c4:]], axis=1).astype(BF16)
    w_dt = jnp.zeros((d, DT_PAD), F32).at[:, :2 * heads].set(w[:, c1:c2])
    w_col = jnp.concatenate([w[:, c0:c1], w[:, c2:c3], w_dt], axis=1).astype(BF16)

    u_lat, ga, gb = in_projection(x, norm_mix_w[0], sc_m, sh_m, w_row, (s5_w, d, d),
                                  col_major=False, tokens_per_step=512)
    xbc_lat, z_lat, dtr_lat = in_projection(x, norm_mix_w[0], sc_m, sh_m, w_col, (conv_dim, ssd_w, DT_PAD),
                                            col_major=True, tokens_per_step=256)
    lc = ctx.shape[1]
    (u_ctx,) = in_projection(ctx, norm_mix_w[0], ctx_mod[1], ctx_mod[0], w_row[:, :s5_w], (s5_w,),
                             col_major=False, tokens_per_step=lc)
    w_col_ctx = jnp.concatenate([w_col[:, :conv_dim], w_col[:, conv_dim + ssd_w:]], axis=1)
    xbc_ctx, dtr_ctx = in_projection(ctx, norm_mix_w[0], ctx_mod[1], ctx_mod[0], w_col_ctx, (conv_dim, DT_PAD),
                                     col_major=False, tokens_per_step=lc)

    exp_f = _s5_expand(s5_lam_re[0, 0], s5_lam_im[0, 0], s5_log_dt[0, 0], s5_b_re[0, 0], s5_b_im[0, 0],
                       s5_c_re[0, 0], s5_c_im[0, 0], b_)
    exp_b = _s5_expand(s5_lam_re[0, 1], s5_lam_im[0, 1], s5_log_dt[0, 1], s5_b_re[0, 1], s5_b_im[0, 1],
                       s5_c_re[0, 1], s5_c_im[0, 1], b_)
    bexp, cexp, ar, ai = [jnp.stack([f, g]) for f, g in zip(exp_f, exp_b)]
    y_s5 = s5_scan(u_ctx, u_lat, bexp, cexp, ar, ai)
    branch_a = s5_glu_branch(y_s5, u_lat, s5_d[0], s5_glu_w[0].astype(BF16), s5_glu_b[0],
                             w_branch_a[0].astype(BF16))

    dt_bias_row = jnp.zeros((1, DT_PAD), F32).at[0, :2 * heads].set(ssd_dt_bias[0].reshape(-1))
    a_row = jnp.zeros((1, LANES), F32).at[0, :2 * heads].set(-jnp.exp(ssd_a_log[0].astype(F32)).reshape(-1))
    xc_lat, dt_lat = ssd_prep(xbc_lat, dtr_lat, ssd_conv_w[0], ssd_conv_b[0], dt_bias_row)
    xc_ctx, dt_ctx = ssd_prep(xbc_ctx, dtr_ctx, ssd_conv_w[0], ssd_conv_b[0], dt_bias_row)
    s_f = ssd_scan(xc_ctx, dt_ctx, a_row, None, reverse=False, width=ssd_w)
    s_b = ssd_scan(xc_ctx, dt_ctx, a_row, None, reverse=True, width=ssd_w)
    y_f = ssd_scan(xc_lat, dt_lat, a_row, s_f, reverse=False, width=ssd_w)
    y_b = ssd_scan(xc_lat, dt_lat, a_row, s_b, reverse=True, width=ssd_w)
    d_row = jnp.repeat(ssd_d[0].astype(F32), SSD_HEADDIM).reshape(1, ssd_w)
    branch_b = ssd_out_branch(y_f, y_b, xc_lat, z_lat, d_row, ssd_norm_w[0], w_branch_b[0].astype(BF16))

    branch_b_rm = branch_b.reshape(b_, GRID_W, l // GRID_W, d).transpose(0, 2, 1, 3).reshape(b_, l, d)

    rw = jnp.zeros((d, ROUTER_PAD), F32).at[:, :N_EXPERTS].set(router_w[0])
    rw3 = jnp.stack(_split3(rw))
    rb = jnp.zeros((1, ROUTER_PAD), F32).at[0, :N_EXPERTS].set(router_b[0])
    h1, xn, ri, rg, cnt = merge_router(x, ga, gb, branch_a, branch_b_rm, g_m, w_out[0].astype(BF16),
                                       norm_ffn_w[0], sc_f, sh_f, rw3, rb)
    n = b_ * l
    counts = cnt[0, :N_EXPERTS].astype(jnp.int32)
    slot_start, block_expert, n_used, n_slots = _slot_tables(counts, n * TOP_K)
    ri = ri.reshape(n, ROUTER_PAD)
    pair_slot = slot_start[ri[:, :TOP_K]] + ri[:, TOP_K:2 * TOP_K]
    slot_token = jnp.zeros((n_slots,), jnp.int32).at[pair_slot.reshape(-1)].set(
        jnp.arange(n * TOP_K, dtype=jnp.int32) // TOP_K, unique_indices=True)
    x_sorted = xn.reshape(n, d)[slot_token]
    y_sorted = expert_blocks(block_expert, n_used, x_sorted,
                             moe_w_gate[0], moe_b_gate[0][:, None, :],
                             moe_w_up[0], moe_b_up[0][:, None, :],
                             moe_w_down[0], moe_b_down[0][:, None, :])
    y4 = y_sorted.at[pair_slot.T].get(unique_indices=True).reshape(TOP_K, b_, l, d)

    return final_combine_norm(h1, y4, rg, g_f, final_norm_w)
```

```python
import functools
import math

import jax
import jax.numpy as jnp
from jax import lax
from jax.experimental import pallas as pl
from jax.experimental.pallas import tpu as pltpu

F32 = jnp.float32
BF16 = jnp.bfloat16

EPS = 1e-6
GRID_W = 64
LANES = 128

S5_GROUP = 16
S5_STATE = 64
S5_COLS = 256
S5_CHUNK = 128

SSD_HEADDIM = 64
SSD_GROUPS = 8
SSD_STATE = 128
SSD_CHUNK = 128
DT_PAD = 256
MASKED_EXPONENT = -1e30

N_EXPERTS = 32
TOP_K = 4
SWIGLU_ALPHA = 1.702
SWIGLU_LIMIT = 7.0
MOE_BLOCK = 512
ROUTER_PAD = 128

VMEM_LIMIT_BYTES = 56 * 1024 * 1024


def _cparams(sem):
    return pltpu.CompilerParams(dimension_semantics=sem, vmem_limit_bytes=VMEM_LIMIT_BYTES)


def _dot(a, b):
    return jnp.dot(a, b, preferred_element_type=F32)


def _split3(x):
    hi = x.astype(BF16)
    r = x - hi.astype(F32)
    mid = r.astype(BF16)
    lo = (r - mid.astype(F32)).astype(BF16)
    return hi, mid, lo


def _sigmoid(x):
    return 0.5 * (1.0 + jnp.tanh(0.5 * x))


def _silu(x):
    return x * _sigmoid(x)


def _softplus(x):
    return jnp.maximum(x, 0.0) + jnp.log1p(jnp.exp(-jnp.abs(x)))


def _gelu_tanh(x):
    c = math.sqrt(2.0 / math.pi)
    return 0.5 * x * (1.0 + jnp.tanh(c * (x + 0.044715 * (x * x * x))))


def _rms(x):
    return x * lax.rsqrt(jnp.mean(x * x, axis=-1, keepdims=True) + EPS)


def _ada_kernel(c_ref, w_ref, b_ref, o_ref):
    s = _silu(c_ref[...]).astype(BF16)
    o_ref[...] = _dot(s, w_ref[...].astype(BF16)) + b_ref[...]


def ada_modulation(cond, w, b):
    n, d = cond.shape
    cols = w.shape[1]
    tn = 1024
    return pl.pallas_call(
        _ada_kernel,
        grid=(cols // tn,),
        in_specs=[pl.BlockSpec((n, d), lambda j: (0, 0)),
                  pl.BlockSpec((d, tn), lambda j: (0, j)),
                  pl.BlockSpec((1, tn), lambda j: (0, j))],
        out_specs=pl.BlockSpec((n, tn), lambda j: (0, j)),
        out_shape=jax.ShapeDtypeStruct((n, cols), F32),
        compiler_params=_cparams(("arbitrary",)),
        name="ada_modulation",
    )(cond, w, b.reshape(1, cols))


def _inproj_kernel(x_ref, nw_ref, sc_ref, sh_ref, w_ref, *rest, d_model, widths):
    o_refs = rest[:len(widths)]
    u_ref = rest[len(widths)]
    r = x_ref.shape[1]
    k = x_ref.shape[2] // d_model
    nw = nw_ref[...]
    sc = 1.0 + sc_ref[0]
    sh = sh_ref[0]
    for c in range(k):
        xs = x_ref[0, :, c * d_model:(c + 1) * d_model]
        u_ref[c * r:(c + 1) * r, :] = ((_rms(xs) * nw) * sc + sh).astype(BF16)
    off = 0
    for o_ref, wd in zip(o_refs, widths):
        step = min(wd, 512)
        for n0 in range(0, wd, step):
            o_ref[0, :, n0:n0 + step] = _dot(u_ref[...], w_ref[:, off + n0:off + n0 + step]).astype(o_ref.dtype)
        off += wd


def in_projection(x, norm_w, scale, shift, w, widths, dtypes, *, col_major, tokens_per_step):
    b_, l, d = x.shape
    if col_major:
        rows = l // GRID_W
        k = tokens_per_step // rows
        xv = x.reshape(b_, rows, GRID_W * d)
        x_spec = pl.BlockSpec((1, rows, k * d), lambda b, i: (b, 0, i))
        steps = GRID_W // k
    else:
        xv = x
        x_spec = pl.BlockSpec((1, tokens_per_step, d), lambda b, i: (b, i, 0))
        steps = l // tokens_per_step
    vec = lambda: pl.BlockSpec((1, 1, d), lambda b, i: (b, 0, 0))
    return pl.pallas_call(
        functools.partial(_inproj_kernel, d_model=d, widths=tuple(widths)),
        grid=(b_, steps),
        in_specs=[x_spec,
                  pl.BlockSpec((1, d), lambda b, i: (0, 0)),
                  vec(), vec(),
                  pl.BlockSpec(w.shape, lambda b, i: (0, 0), pipeline_mode=pl.Buffered(1))],
        out_specs=[pl.BlockSpec((1, tokens_per_step, wd), lambda b, i: (b, i, 0)) for wd in widths],
        out_shape=[jax.ShapeDtypeStruct((b_, l, wd), dt) for wd, dt in zip(widths, dtypes)],
        scratch_shapes=[pltpu.VMEM((tokens_per_step, d), BF16)],
        compiler_params=_cparams(("parallel", "arbitrary")),
        name="in_projection_col" if col_major else "in_projection_row",
    )(xv, norm_w.reshape(1, d), scale, shift, w)


def _s5_kernel(uc_ref, ul_ref, bexp_ref, cexp_ref, ar_ref, ai_ref, y_ref, *scratch, n_ctx, n_chunks):
    nslab = (len(scratch) - 2) // 2
    ubufs, ybufs = scratch[:nslab], scratch[nslab:2 * nslab]
    buf, state = scratch[2 * nslab:]
    d = pl.program_id(0)
    s = pl.program_id(2)
    nb, tl = ul_ref.shape[0], ul_ref.shape[1]
    half = state.shape[1] // 2
    p = s % 2
    is_ctx = jnp.minimum(s, n_chunks - 1) < n_ctx

    @pl.when(s == 0)
    def _():
        buf[...] = jnp.zeros_like(buf)
        state[...] = jnp.zeros_like(state)

    y_tm = _dot(buf[p].astype(BF16), cexp_ref[0, 0])
    for k, ybuf in enumerate(ybufs):
        ybuf[...] = y_tm[:, k * LANES:(k + 1) * LANES]
    for b in range(nb):
        for k, ybuf in enumerate(ybufs):
            y_ref[0, b, :, k * LANES:(k + 1) * LANES] = ybuf[pl.ds(b, tl, stride=nb), :].astype(y_ref.dtype)

    for b in range(nb):
        ub = jnp.where(is_ctx, uc_ref[b], ul_ref[b])
        for k, ubuf in enumerate(ubufs):
            ubuf[pl.ds(b, tl, stride=nb), :] = ub[:, k * LANES:(k + 1) * LANES]
    u_tm = jnp.concatenate([ubuf[...] for ubuf in ubufs], axis=1)
    buf[p] = _dot(u_tm.astype(BF16), bexp_ref[0, 0])

    ar = ar_ref[0, 0]
    ai = ai_ref[0, 0]
    xr = state[:, 0:half]
    xi = state[:, half:2 * half]
    q = 1 - p
    for t in range(tl):
        tok = jnp.where(d == 0, t, tl - 1 - t)
        row = pl.multiple_of(tok * nb, nb)
        br = buf[q, pl.ds(row, nb), 0:half]
        bi = buf[q, pl.ds(row, nb), half:2 * half]
        xr, xi = ar * xr - ai * xi + br, ar * xi + ai * xr + bi
        buf[q, pl.ds(row, nb), 0:half] = xr
        buf[q, pl.ds(row, nb), half:2 * half] = xi
    state[:, 0:half] = xr
    state[:, half:2 * half] = xi


def s5_scan(u_ctx, u_lat, bexp, cexp, ar, ai):
    b_, l_lat, w = u_lat.shape
    l_ctx = u_ctx.shape[1]
    tl = S5_CHUNK
    n_ctx, n_lat = l_ctx // tl, l_lat // tl
    n_chunks = n_ctx + n_lat
    ncb = w // S5_COLS
    ns = bexp.shape[-1]

    def ctx_idx(d, k):
        return jnp.where(d == 0, jnp.minimum(k, n_ctx - 1), jnp.maximum(n_ctx - 1 - k, 0))

    def lat_idx(d, k):
        kk = jnp.clip(k - n_ctx, 0, n_lat - 1)
        return jnp.where(d == 0, kk, n_lat - 1 - kk)

    in_pos = lambda s: jnp.minimum(s, n_chunks - 1)
    out_pos = lambda s: jnp.maximum(s - 2, 0)
    return pl.pallas_call(
        functools.partial(_s5_kernel, n_ctx=n_ctx, n_chunks=n_chunks),
        grid=(2, ncb, n_chunks + 2),
        in_specs=[pl.BlockSpec((b_, tl, S5_COLS), lambda d, j, s: (0, ctx_idx(d, in_pos(s)), j)),
                  pl.BlockSpec((b_, tl, S5_COLS), lambda d, j, s: (0, lat_idx(d, in_pos(s)), j)),
                  pl.BlockSpec((1, 1, S5_COLS, ns), lambda d, j, s: (d, j, 0, 0)),
                  pl.BlockSpec((1, 1, ns, S5_COLS), lambda d, j, s: (d, j, 0, 0)),
                  pl.BlockSpec((1, 1, b_, ns // 2), lambda d, j, s: (d, j, 0, 0)),
                  pl.BlockSpec((1, 1, b_, ns // 2), lambda d, j, s: (d, j, 0, 0))],
        out_specs=pl.BlockSpec((1, b_, tl, S5_COLS), lambda d, j, s: (d, 0, lat_idx(d, out_pos(s)), j)),
        out_shape=jax.ShapeDtypeStruct((2, b_, l_lat, w), BF16),
        scratch_shapes=([pltpu.VMEM((tl * b_, LANES), F32)] * (2 * (S5_COLS // LANES))
                        + [pltpu.VMEM((2, tl * b_, ns), F32), pltpu.VMEM((b_, ns), F32)]),
        compiler_params=_cparams(("parallel", "parallel", "arbitrary")),
        name="s5_scan",
    )(u_ctx, u_lat, bexp, cexp, ar, ai)


def _s5_expand(lam_re, lam_im, log_dt, b_re, b_im, c_re, c_im, batch):
    lam = lax.complex(lam_re.astype(F32), lam_im.astype(F32))
    step = jnp.exp(log_dt.astype(F32))[:, None]
    a_bar = jnp.exp(lam * step)
    b_mat = lax.complex(b_re.astype(F32), b_im.astype(F32))
    b_bar = ((a_bar - 1.0) / lam)[:, :, None] * b_mat
    g, p, h = b_bar.shape
    gl = S5_COLS // h
    ncb = g // gl
    eye = jnp.eye(gl, dtype=F32)

    def expand_b(t):
        return jnp.einsum('jgph,gk->jghkp', t.reshape(ncb, gl, p, h), eye).reshape(ncb, gl * h, gl * p)

    def expand_c(t):
        return jnp.einsum('jghp,gk->jgpkh', t.reshape(ncb, gl, h, p), eye).reshape(ncb, gl * p, gl * h)

    bexp = jnp.concatenate([expand_b(jnp.real(b_bar)), expand_b(jnp.imag(b_bar))], axis=-1)
    cexp = jnp.concatenate([expand_c(c_re.astype(F32)), -expand_c(c_im.astype(F32))], axis=1)
    ar = jnp.broadcast_to(jnp.real(a_bar).reshape(ncb, 1, gl * p), (ncb, batch, gl * p))
    ai = jnp.broadcast_to(jnp.imag(a_bar).reshape(ncb, 1, gl * p), (ncb, batch, gl * p))
    return bexp.astype(BF16), cexp.astype(BF16), ar, ai


def _s5_glu_kernel(yf_ref, yb_ref, u_ref, d_ref, gw_ref, gb_ref, wa_ref, o_ref):
    y = yf_ref[0, 0].astype(F32) + yb_ref[0, 0].astype(F32) + d_ref[...] * u_ref[0]
    g = _gelu_tanh(y)
    gate = _sigmoid(_dot(g.astype(BF16), gw_ref[...]) + gb_ref[...])
    o_ref[0] = _dot((g * gate).astype(BF16), wa_ref[...]).astype(o_ref.dtype)


def s5_glu_branch(y2, u, s5_d, glu_w, glu_b, w_branch):
    b_, l, w = u.shape
    dm = w_branch.shape[1]
    tm = 256
    row = lambda: pl.BlockSpec((1, w), lambda b, i: (0, 0))
    return pl.pallas_call(
        _s5_glu_kernel,
        grid=(b_, l // tm),
        in_specs=[pl.BlockSpec((1, 1, tm, w), lambda b, i: (0, b, i, 0)),
                  pl.BlockSpec((1, 1, tm, w), lambda b, i: (1, b, i, 0)),
                  pl.BlockSpec((1, tm, w), lambda b, i: (b, i, 0)),
                  row(),
                  pl.BlockSpec((w, w), lambda b, i: (0, 0)),
                  row(),
                  pl.BlockSpec((w, dm), lambda b, i: (0, 0))],
        out_specs=pl.BlockSpec((1, tm, dm), lambda b, i: (b, i, 0)),
        out_shape=jax.ShapeDtypeStruct((b_, l, dm), BF16),
        compiler_params=_cparams(("parallel", "arbitrary")),
        name="s5_glu_branch",
    )(y2, y2, u, s5_d.reshape(1, w), glu_w, glu_b.reshape(1, w), w_branch)


def _ssd_prep_kernel(prev_ref, cur_ref, next_ref, dt_ref, cw_ref, cb_ref, dtb_ref, xo_ref, dto_ref, *, kw):
    i = pl.program_id(1)
    last = pl.num_programs(1) - 1
    tl = cur_ref.shape[1]
    halo = prev_ref.shape[1]
    pad = kw // 2
    prev = jnp.where(i == 0, 0.0, prev_ref[0])
    nxt = jnp.where(i == last, 0.0, next_ref[0])
    ext = jnp.concatenate([prev, cur_ref[0], nxt], axis=0)
    acc = cb_ref[...] + jnp.zeros((tl, cur_ref.shape[2]), F32)
    rows = ext.shape[0]
    for k in range(kw):
        shifted = ext if k == pad else pltpu.roll(ext, (pad - k) % rows, 0)
        acc = acc + shifted[halo:halo + tl, :] * cw_ref[k:k + 1, :]
    xo_ref[0] = _silu(acc)
    dto_ref[0] = _softplus(dt_ref[0] + dtb_ref[...])


def ssd_prep(xbc, dt_raw, conv_w, conv_b, dt_bias_row):
    b_, l, c = xbc.shape
    tl = min(256, l)
    halo = 8
    hb = tl // halo
    nh = l // halo
    kw = conv_w.shape[0]
    cw = jnp.zeros((8, c), F32).at[:kw].set(conv_w)
    dp = dt_raw.shape[2]
    return pl.pallas_call(
        functools.partial(_ssd_prep_kernel, kw=kw),
        grid=(b_, l // tl),
        in_specs=[pl.BlockSpec((1, halo, c), lambda b, i: (b, jnp.maximum(i * hb - 1, 0), 0)),
                  pl.BlockSpec((1, tl, c), lambda b, i: (b, i, 0)),
                  pl.BlockSpec((1, halo, c), lambda b, i: (b, jnp.minimum((i + 1) * hb, nh - 1), 0)),
                  pl.BlockSpec((1, tl, dp), lambda b, i: (b, i, 0)),
                  pl.BlockSpec((8, c), lambda b, i: (0, 0)),
                  pl.BlockSpec((1, c), lambda b, i: (0, 0)),
                  pl.BlockSpec((1, dp), lambda b, i: (0, 0))],
        out_specs=[pl.BlockSpec((1, tl, c), lambda b, i: (b, i, 0)),
                   pl.BlockSpec((1, tl, dp), lambda b, i: (b, i, 0))],
        out_shape=[jax.ShapeDtypeStruct((b_, l, c), F32),
                   jax.ShapeDtypeStruct((b_, l, dp), F32)],
        compiler_params=_cparams(("parallel", "arbitrary")),
        name="ssd_prep",
    )(xbc, xbc, xbc, dt_raw, cw, conv_b.reshape(1, c), dt_bias_row)


def _ssd_scan_kernel(*refs, reverse, width, col0, with_output):
    if with_output:
        x_ref, dt_ref, a_ref, e_ref, s0_ref, y_ref, state, bt_scr = refs
    else:
        x_ref, dt_ref, a_ref, e_ref, so_ref, state, bt_scr = refs
    i = pl.program_id(1)
    q = x_ref.shape[1]
    n = SSD_STATE
    p = SSD_HEADDIM
    ng = SSD_GROUPS
    gw = width // ng
    hpg = gw // p

    @pl.when(i == 0)
    def _():
        state[...] = s0_ref[0] if with_output else jnp.zeros_like(state)

    dt = dt_ref[0][:, :LANES]
    dta = dt * a_ref[...]
    ri = lax.broadcasted_iota(jnp.int32, (q, q), 0)
    ci = lax.broadcasted_iota(jnp.int32, (q, q), 1)
    mask = (ri <= ci) if reverse else (ri >= ci)
    tri = mask.astype(BF16)
    hi, mid, lo = _split3(dta)
    acum = _dot(tri, hi) + _dot(tri, mid) + _dot(tri, lo)

    def expand(v):
        h_, m_, l_ = _split3(v)
        return _dot(jnp.concatenate([h_, m_], axis=1), e_ref[...]) + _dot(l_, e_ref[0:LANES, :])

    acum_e = expand(acum)
    dt_e = expand(dt)
    last = 0 if reverse else q - 1
    tot_e = acum_e[last:last + 1, :]
    xdt = x_ref[0, :, 0:width] * dt_e
    xw = (xdt * jnp.exp(tot_e - acum_e)).astype(BF16)
    dec_e = jnp.exp(tot_e)
    for g in range(ng):
        bt_scr[g] = x_ref[0, :, width + g * n: width + (g + 1) * n].T.astype(BF16)

    if with_output:
        xdt_bf = xdt.astype(BF16)
        eac = jnp.exp(acum_e)
        acum_t = acum.T
        head_of_lane = lax.broadcasted_iota(jnp.int32, (1, gw), 1) // p
        for g in range(ng):
            sl = slice(g * gw, (g + 1) * gw)
            cmat = x_ref[0, :, width + (ng + g) * n: width + (ng + g + 1) * n].astype(BF16)
            cb = _dot(cmat, bt_scr[g])
            yoff = _dot(cmat, state[g].astype(BF16)) * eac[:, sl]
            xg = xdt_bf[:, sl]
            gms, rhs = [], []
            for r in range(hpg):
                hc = col0 + g * hpg + r
                diff = acum[:, hc:hc + 1] - acum_t[hc:hc + 1, :]
                gms.append((cb * jnp.exp(jnp.where(mask, diff, MASKED_EXPONENT))).astype(BF16))
                rhs.append(jnp.where(head_of_lane == r, xg, jnp.zeros_like(xg)))
            yd = _dot(jnp.concatenate(gms, axis=1), jnp.concatenate(rhs, axis=0))
            y_ref[0, :, sl] = (yd + yoff).astype(y_ref.dtype)

    for g in range(ng):
        sl = slice(g * gw, (g + 1) * gw)
        state[g] = state[g] * dec_e[:, sl] + _dot(bt_scr[g], xw[:, sl])
    if not with_output:
        so_ref[0] = state[...]


def ssd_scan(xc, dt, a_row, init_state, *, reverse, width):
    b_, l, c = xc.shape
    q = SSD_CHUNK
    nc = l // q
    dp = dt.shape[2]
    heads = width // SSD_HEADDIM
    col0 = heads if reverse else 0
    with_output = init_state is not None
    gw = width // SSD_GROUPS
    col = lax.broadcasted_iota(jnp.int32, (2 * LANES, width), 0) % LANES
    expand_mat = (col == col0 + lax.broadcasted_iota(jnp.int32, (2 * LANES, width), 1) // SSD_HEADDIM).astype(BF16)
    chunk = (lambda i: nc - 1 - i) if reverse else (lambda i: i)
    state_spec = pl.BlockSpec((1, SSD_GROUPS, SSD_STATE, gw), lambda b, i: (b, 0, 0, 0))
    in_specs = [pl.BlockSpec((1, q, c), lambda b, i: (b, chunk(i), 0)),
                pl.BlockSpec((1, q, dp), lambda b, i: (b, chunk(i), 0)),
                pl.BlockSpec((1, LANES), lambda b, i: (0, 0)),
                pl.BlockSpec((2 * LANES, width), lambda b, i: (0, 0))]
    args = [xc, dt, a_row, expand_mat]
    if with_output:
        in_specs.append(state_spec)
        args.append(init_state)
        out_specs = pl.BlockSpec((1, q, width), lambda b, i: (b, chunk(i), 0))
        out_shape = jax.ShapeDtypeStruct((b_, l, width), BF16)
    else:
        out_specs = state_spec
        out_shape = jax.ShapeDtypeStruct((b_, SSD_GROUPS, SSD_STATE, gw), F32)
    return pl.pallas_call(
        functools.partial(_ssd_scan_kernel, reverse=reverse, width=width, col0=col0, with_output=with_output),
        grid=(b_, nc),
        in_specs=in_specs,
        out_specs=out_specs,
        out_shape=out_shape,
        scratch_shapes=[pltpu.VMEM((SSD_GROUPS, SSD_STATE, gw), F32),
                        pltpu.VMEM((SSD_GROUPS, SSD_STATE, q), BF16)],
        compiler_params=_cparams(("parallel", "arbitrary")),
        name=("ssd_scan" if with_output else "ssd_state") + ("_bwd" if reverse else "_fwd"),
    )(*args)


def _ssd_out_kernel(yf_ref, yb_ref, x_ref, z_ref, d_ref, nw_ref, wb_ref, o_ref):
    y = ((yf_ref[0].astype(F32) + yb_ref[0].astype(F32) + d_ref[...] * x_ref[0])
         * _silu(z_ref[0].astype(F32)))
    w = y.shape[1]
    gw = w // SSD_GROUPS
    parts = [_rms(y[:, g * gw:(g + 1) * gw]) for g in range(SSD_GROUPS)]
    yn = (jnp.concatenate(parts, axis=1) * nw_ref[...]).astype(BF16)
    o_ref[0] = _dot(yn, wb_ref[...]).astype(o_ref.dtype)


def ssd_out_branch(yf, yb, xc_lat, z, d_row, norm_w, w_branch):
    b_, l, w = yf.shape
    dm = w_branch.shape[1]
    tm = 256
    blk = lambda: pl.BlockSpec((1, tm, w), lambda b, i: (b, i, 0))
    row = lambda: pl.BlockSpec((1, w), lambda b, i: (0, 0))
    return pl.pallas_call(
        _ssd_out_kernel,
        grid=(b_, l // tm),
        in_specs=[blk(), blk(), blk(), blk(), row(), row(),
                  pl.BlockSpec((w, dm), lambda b, i: (0, 0))],
        out_specs=pl.BlockSpec((1, tm, dm), lambda b, i: (b, i, 0)),
        out_shape=jax.ShapeDtypeStruct((b_, l, dm), BF16),
        compiler_params=_cparams(("parallel", "arbitrary")),
        name="ssd_out_branch",
    )(yf, yb, xc_lat, z, d_row, norm_w.reshape(1, w), w_branch)


def _merge_router_kernel(x_ref, ga_ref, gb_ref, ba_ref, bb_ref, gm_ref, wo_ref, nw_ref, sc_ref, sh_ref,
                         rw_ref, rb_ref, tri_ref, h_ref, xn_ref, ri_ref, rg_ref, cnt_ref, carry):
    @pl.when(jnp.logical_and(pl.program_id(0) == 0, pl.program_id(1) == 0))
    def _():
        carry[...] = jnp.zeros_like(carry)

    m = (_sigmoid(ga_ref[0].astype(F32)) * ba_ref[0].astype(F32)
         + _sigmoid(gb_ref[0].astype(F32)) * bb_ref[0].astype(F32))
    h = x_ref[0] + gm_ref[0] * _dot(m.astype(BF16), wo_ref[...])
    h_ref[0] = h
    u = (_rms(h) * nw_ref[...]) * (1.0 + sc_ref[0]) + sh_ref[0]
    xn_ref[0] = u.astype(BF16)
    uh, um, ul = _split3(u)
    wh, wm, wl = rw_ref[0], rw_ref[1], rw_ref[2]
    lg = (_dot(uh, wh) + (_dot(uh, wm) + _dot(um, wh))
          + (_dot(uh, wl) + _dot(um, wm) + _dot(ul, wh))) + rb_ref[...]

    tm, rp = lg.shape
    lane = lax.broadcasted_iota(jnp.int32, (tm, rp), 1)
    lane_f = lane.astype(F32)
    neg = jnp.float32(-jnp.inf)
    cur = jnp.where(lane < N_EXPERTS, lg, neg)
    onehots, vals, idxs = [], [], []
    for _ in range(TOP_K):
        mx = jnp.max(cur, axis=-1, keepdims=True)
        idx = jnp.min(jnp.where(cur == mx, lane_f, float(rp)), axis=-1, keepdims=True)
        oh = lane_f == idx
        onehots.append(oh)
        vals.append(mx)
        idxs.append(idx)
        cur = jnp.where(oh, neg, cur)
    exps = [jnp.exp(v - vals[0]) for v in vals]
    denom = exps[0] + exps[1] + exps[2] + exps[3]
    oh_all = (onehots[0] | onehots[1] | onehots[2] | onehots[3]).astype(F32)
    before = _dot(tri_ref[...], oh_all.astype(BF16)) + carry[...]
    ri = jnp.zeros((tm, rp), jnp.int32)
    rg = jnp.zeros((tm, rp), F32)
    for k in range(TOP_K):
        rank = jnp.sum(jnp.where(onehots[k], before, 0.0), axis=-1, keepdims=True).astype(jnp.int32)
        ri = jnp.where(lane == k, idxs[k].astype(jnp.int32), ri)
        ri = jnp.where(lane == TOP_K + k, rank, ri)
        rg = jnp.where(lane == k, exps[k] / denom, rg)
    ri_ref[0] = ri
    rg_ref[0] = rg
    carry[...] = carry[...] + jnp.sum(oh_all, axis=0, keepdims=True)
    cnt_ref[...] = jnp.broadcast_to(carry[...], cnt_ref.shape)


def merge_router(x, ga, gb, branch_a, branch_b, gate_m, w_out, norm_w, scale, shift, router_w3, router_b):
    b_, l, d = x.shape
    tm = 512
    rp = router_w3.shape[2]
    tri = (lax.broadcasted_iota(jnp.int32, (tm, tm), 0) > lax.broadcasted_iota(jnp.int32, (tm, tm), 1)).astype(BF16)
    blk = lambda w=d: pl.BlockSpec((1, tm, w), lambda b, i: (b, i, 0))
    vec = lambda: pl.BlockSpec((1, 1, d), lambda b, i: (b, 0, 0))
    return pl.pallas_call(
        _merge_router_kernel,
        grid=(b_, l // tm),
        in_specs=[blk(), blk(), blk(), blk(), blk(), vec(),
                  pl.BlockSpec((d, d), lambda b, i: (0, 0)),
                  pl.BlockSpec((1, d), lambda b, i: (0, 0)),
                  vec(), vec(),
                  pl.BlockSpec((3, d, rp), lambda b, i: (0, 0, 0)),
                  pl.BlockSpec((1, rp), lambda b, i: (0, 0)),
                  pl.BlockSpec((tm, tm), lambda b, i: (0, 0))],
        out_specs=[blk(), blk(), blk(rp), blk(rp),
                   pl.BlockSpec((8, rp), lambda b, i: (0, 0))],
        out_shape=[jax.ShapeDtypeStruct((b_, l, d), F32),
                   jax.ShapeDtypeStruct((b_, l, d), BF16),
                   jax.ShapeDtypeStruct((b_, l, rp), jnp.int32),
                   jax.ShapeDtypeStruct((b_, l, rp), F32),
                   jax.ShapeDtypeStruct((8, rp), F32)],
        scratch_shapes=[pltpu.VMEM((1, rp), F32)],
        compiler_params=_cparams(("arbitrary", "arbitrary")),
        name="merge_router",
    )(x, ga, gb, branch_a, branch_b, gate_m, w_out, norm_w.reshape(1, d), scale, shift,
      router_w3, router_b, tri)


def _expert_kernel(be_ref, nu_ref, x_ref, wg_ref, bg_ref, wu_ref, bu_ref, wd_ref, bd_ref, o_ref,
                   wg_bf, wu_bf, wd_bf):
    i = pl.program_id(0)
    new_expert = jnp.logical_or(i == 0, be_ref[i] != be_ref[jnp.maximum(i - 1, 0)])

    @pl.when(jnp.logical_and(new_expert, i < nu_ref[0]))
    def _():
        wg_bf[...] = wg_ref[0].astype(BF16)
        wu_bf[...] = wu_ref[0].astype(BF16)
        wd_bf[...] = wd_ref[0].astype(BF16)

    @pl.when(i < nu_ref[0])
    def _():
        xb = x_ref[...]
        gate = jnp.minimum(_dot(xb, wg_bf[...]) + bg_ref[0], SWIGLU_LIMIT)
        up = jnp.clip(_dot(xb, wu_bf[...]) + bu_ref[0], -SWIGLU_LIMIT, SWIGLU_LIMIT)
        act = (up + 1.0) * gate * _sigmoid(SWIGLU_ALPHA * gate)
        o_ref[...] = (_dot(act.astype(BF16), wd_bf[...]) + bd_ref[0]).astype(o_ref.dtype)

    @pl.when(i >= nu_ref[0])
    def _():
        o_ref[...] = jnp.zeros_like(o_ref)


def expert_blocks(block_expert, n_used, x_sorted, wg, bg, wu, bu, wd, bd):
    n_slots, d = x_sorted.shape
    de = wg.shape[2]
    nblk = n_slots // MOE_BLOCK
    wspec = lambda a, b: pl.BlockSpec((1, a, b), lambda i, be, nu: (be[i], 0, 0))
    grid_spec = pltpu.PrefetchScalarGridSpec(
        num_scalar_prefetch=2,
        grid=(nblk,),
        in_specs=[pl.BlockSpec((MOE_BLOCK, d), lambda i, be, nu: (i, 0)),
                  wspec(d, de), wspec(1, de), wspec(d, de), wspec(1, de), wspec(de, d), wspec(1, d)],
        out_specs=pl.BlockSpec((MOE_BLOCK, d), lambda i, be, nu: (i, 0)),
        scratch_shapes=[pltpu.VMEM((d, de), BF16), pltpu.VMEM((d, de), BF16), pltpu.VMEM((de, d), BF16)],
    )
    return pl.pallas_call(
        _expert_kernel,
        grid_spec=grid_spec,
        out_shape=jax.ShapeDtypeStruct((n_slots, d), BF16),
        compiler_params=_cparams(("arbitrary",)),
        name="expert_blocks",
    )(block_expert, n_used, x_sorted, wg, bg, wu, bu, wd, bd)


def _final_kernel(h_ref, y_ref, rg_ref, g_ref, w_ref, o_ref):
    rg = rg_ref[0]
    ffn = y_ref[0, 0].astype(F32) * rg[:, 0:1]
    for k in range(1, TOP_K):
        ffn = ffn + y_ref[k, 0].astype(F32) * rg[:, k:k + 1]
    h = h_ref[0] + g_ref[0] * ffn
    o_ref[0] = _rms(h) * w_ref[...]


def final_combine_norm(h, y4, rg, gate_f, final_w):
    b_, l, d = h.shape
    tm = 512
    blk = lambda w=d: pl.BlockSpec((1, tm, w), lambda b, i: (b, i, 0))
    return pl.pallas_call(
        _final_kernel,
        grid=(b_, l // tm),
        in_specs=[blk(),
                  pl.BlockSpec((TOP_K, 1, tm, d), lambda b, i: (0, b, i, 0)),
                  blk(rg.shape[2]),
                  pl.BlockSpec((1, 1, d), lambda b, i: (b, 0, 0)),
                  pl.BlockSpec((1, d), lambda b, i: (0, 0))],
        out_specs=blk(),
        out_shape=jax.ShapeDtypeStruct((b_, l, d), F32),
        compiler_params=_cparams(("parallel", "arbitrary")),
        name="final_combine_norm",
    )(h, y4, rg, gate_f, final_w.reshape(1, d))


def _slot_tables(counts, n_pairs):
    padded = (counts + MOE_BLOCK - 1) // MOE_BLOCK * MOE_BLOCK
    padded_end = jnp.cumsum(padded)
    n_blocks = n_pairs // MOE_BLOCK + N_EXPERTS
    block_start = jnp.arange(n_blocks, dtype=jnp.int32) * MOE_BLOCK
    block_expert = jnp.minimum(jnp.sum(block_start[:, None] >= padded_end[None, :], axis=1),
                               N_EXPERTS - 1).astype(jnp.int32)
    n_used = (padded_end[-1] // MOE_BLOCK).astype(jnp.int32).reshape(1)
    return (padded_end - padded).astype(jnp.int32), block_expert, n_used, n_blocks * MOE_BLOCK


def kernel(x, c, ctx, c_ctx, ada_w, ada_b, norm_mix_w, w_in, s5_lam_re, s5_lam_im, s5_log_dt, s5_b_re, s5_b_im, s5_c_re, s5_c_im, s5_d, s5_glu_w, s5_glu_b, ssd_conv_w, ssd_conv_b, ssd_dt_bias, ssd_a_log, ssd_d, ssd_norm_w, w_branch_a, w_branch_b, w_out, norm_ffn_w, router_w, router_b, moe_w_gate, moe_b_gate, moe_w_up, moe_b_up, moe_w_down, moe_b_down, final_norm_w):
    assert ada_w.shape[0] == 1, "single-layer block"
    b_, l, d = x.shape
    s5_w = s5_d.shape[1]
    ssd_w = ssd_d.shape[1] * SSD_HEADDIM
    heads = ssd_d.shape[1]
    conv_dim = ssd_w + 2 * SSD_GROUPS * SSD_STATE
    c0, c1, c2, c3, c4 = (s5_w, s5_w + conv_dim, s5_w + conv_dim + 2 * heads,
                          s5_w + conv_dim + 2 * heads + ssd_w, s5_w + conv_dim + 2 * heads + ssd_w + d)

    cond = jnp.concatenate([c, c_ctx[None, :], jnp.zeros((7, d), F32)], axis=0)
    mod = ada_modulation(cond, ada_w[0], ada_b[0])
    mod6 = mod.reshape(mod.shape[0], 6, d)
    lat_mod = [mod6[:b_, k][:, None, :] for k in range(6)]
    ctx_mod = [jnp.broadcast_to(mod6[b_, k][None, None, :], (b_, 1, d)) for k in range(6)]
    sh_m, sc_m, g_m, sh_f, sc_f, g_f = lat_mod

    w = w_in[0]
    w_row = jnp.concatenate([w[:, :c0], w[:, c3:c4], w[:, c4:]], axis=1).astype(BF16)
    w_dt = jnp.zeros((d, DT_PAD), F32).at[:, :2 * heads].set(w[:, c1:c2])
    w_col = jnp.concatenate([w[:, c0:c1], w[:, c2:c3], w_dt], axis=1).astype(BF16)

    u_lat, ga, gb = in_projection(x, norm_mix_w[0], sc_m, sh_m, w_row, (s5_w, d, d), (F32, BF16, BF16),
                                  col_major=False, tokens_per_step=512)
    xbc_lat, z_lat, dtr_lat = in_projection(x, norm_mix_w[0], sc_m, sh_m, w_col, (conv_dim, ssd_w, DT_PAD),
                                            (F32, BF16, F32),
                                            col_major=True, tokens_per_step=256)
    lc = ctx.shape[1]
    (u_ctx,) = in_projection(ctx, norm_mix_w[0], ctx_mod[1], ctx_mod[0], w_row[:, :s5_w], (s5_w,), (F32,),
                             col_major=False, tokens_per_step=lc)
    w_col_ctx = jnp.concatenate([w_col[:, :conv_dim], w_col[:, conv_dim + ssd_w:]], axis=1)
    xbc_ctx, dtr_ctx = in_projection(ctx, norm_mix_w[0], ctx_mod[1], ctx_mod[0], w_col_ctx, (conv_dim, DT_PAD),
                                     (F32, F32),
                                     col_major=False, tokens_per_step=lc)

    exp_f = _s5_expand(s5_lam_re[0, 0], s5_lam_im[0, 0], s5_log_dt[0, 0], s5_b_re[0, 0], s5_b_im[0, 0],
                       s5_c_re[0, 0], s5_c_im[0, 0], b_)
    exp_b = _s5_expand(s5_lam_re[0, 1], s5_lam_im[0, 1], s5_log_dt[0, 1], s5_b_re[0, 1], s5_b_im[0, 1],
                       s5_c_re[0, 1], s5_c_im[0, 1], b_)
    bexp, cexp, ar, ai = [jnp.stack([f, g]) for f, g in zip(exp_f, exp_b)]
    y_s5 = s5_scan(u_ctx, u_lat, bexp, cexp, ar, ai)
    branch_a = s5_glu_branch(y_s5, u_lat, s5_d[0], s5_glu_w[0].astype(BF16), s5_glu_b[0],
                             w_branch_a[0].astype(BF16))

    dt_bias_row = jnp.zeros((1, DT_PAD), F32).at[0, :2 * heads].set(ssd_dt_bias[0].reshape(-1))
    a_row = jnp.zeros((1, LANES), F32).at[0, :2 * heads].set(-jnp.exp(ssd_a_log[0].astype(F32)).reshape(-1))
    xc_lat, dt_lat = ssd_prep(xbc_lat, dtr_lat, ssd_conv_w[0], ssd_conv_b[0], dt_bias_row)
    xc_ctx, dt_ctx = ssd_prep(xbc_ctx, dtr_ctx, ssd_conv_w[0], ssd_conv_b[0], dt_bias_row)
    s_f = ssd_scan(xc_ctx, dt_ctx, a_row, None, reverse=False, width=ssd_w)
    s_b = ssd_scan(xc_ctx, dt_ctx, a_row, None, reverse=True, width=ssd_w)
    y_f = ssd_scan(xc_lat, dt_lat, a_row, s_f, reverse=False, width=ssd_w)
    y_b = ssd_scan(xc_lat, dt_lat, a_row, s_b, reverse=True, width=ssd_w)
    d_row = jnp.repeat(ssd_d[0].astype(F32), SSD_HEADDIM).reshape(1, ssd_w)
    branch_b = ssd_out_branch(y_f, y_b, xc_lat, z_lat, d_row, ssd_norm_w[0], w_branch_b[0].astype(BF16))

    branch_b_rm = branch_b.reshape(b_, GRID_W, l // GRID_W, d).transpose(0, 2, 1, 3).reshape(b_, l, d)

    rw = jnp.zeros((d, ROUTER_PAD), F32).at[:, :N_EXPERTS].set(router_w[0])
    rw3 = jnp.stack(_split3(rw))
    rb = jnp.zeros((1, ROUTER_PAD), F32).at[0, :N_EXPERTS].set(router_b[0])
    h1, xn, ri, rg, cnt = merge_router(x, ga, gb, branch_a, branch_b_rm, g_m, w_out[0].astype(BF16),
                                       norm_ffn_w[0], sc_f, sh_f, rw3, rb)
    n = b_ * l
    counts = cnt[0, :N_EXPERTS].astype(jnp.int32)
    slot_start, block_expert, n_used, n_slots = _slot_tables(counts, n * TOP_K)
    ri = ri.reshape(n, ROUTER_PAD)
    pair_slot = slot_start[ri[:, :TOP_K]] + ri[:, TOP_K:2 * TOP_K]
    slot_token = jnp.zeros((n_slots,), jnp.int32).at[pair_slot.reshape(-1)].set(
        jnp.arange(n * TOP_K, dtype=jnp.int32) // TOP_K)
    x_sorted = xn.reshape(n, d)[slot_token]
    y_sorted = expert_blocks(block_expert, n_used, x_sorted,
                             moe_w_gate[0], moe_b_gate[0][:, None, :],
                             moe_w_up[0], moe_b_up[0][:, None, :],
                             moe_w_down[0], moe_b_down[0][:, None, :])
    y4 = y_sorted[pair_slot.T].reshape(TOP_K, b_, l, d)

    return final_combine_norm(h1, y4, rg, g_f, final_norm_w)
```

```python
import functools
import math

import jax
import jax.numpy as jnp
from jax import lax
from jax.experimental import pallas as pl
from jax.experimental.pallas import tpu as pltpu

F32 = jnp.float32
BF16 = jnp.bfloat16

EPS = 1e-6
GRID_W = 64
LANES = 128

S5_GROUP = 16
S5_STATE = 64
S5_COLS = 256
S5_CHUNK = 128
S5_SEGMENTS = 1

SSD_HEADDIM = 64
SSD_GROUPS = 8
SSD_STATE = 128
SSD_CHUNK = 128
DT_PAD = 256
MASKED_EXPONENT = -1e30

N_EXPERTS = 32
TOP_K = 4
SWIGLU_ALPHA = 1.702
SWIGLU_LIMIT = 7.0
MOE_BLOCK = 512
ROUTER_PAD = 128

VMEM_LIMIT_BYTES = 56 * 1024 * 1024


def _cparams(sem):
    return pltpu.CompilerParams(dimension_semantics=sem, vmem_limit_bytes=VMEM_LIMIT_BYTES)


def _dot(a, b):
    return jnp.dot(a, b, preferred_element_type=F32)


def _split3(x):
    hi = x.astype(BF16)
    r = x - hi.astype(F32)
    mid = r.astype(BF16)
    lo = (r - mid.astype(F32)).astype(BF16)
    return hi, mid, lo


def _sigmoid(x):
    return 0.5 * (1.0 + jnp.tanh(0.5 * x))


def _silu(x):
    return x * _sigmoid(x)


def _softplus(x):
    return jnp.maximum(x, 0.0) + jnp.log1p(jnp.exp(-jnp.abs(x)))


def _gelu_tanh(x):
    c = math.sqrt(2.0 / math.pi)
    return 0.5 * x * (1.0 + jnp.tanh(c * (x + 0.044715 * (x * x * x))))


def _rms(x):
    return x * lax.rsqrt(jnp.mean(x * x, axis=-1, keepdims=True) + EPS)


def _ada_kernel(c_ref, w_ref, b_ref, o_ref):
    s = _silu(c_ref[...]).astype(BF16)
    o_ref[...] = _dot(s, w_ref[...].astype(BF16)) + b_ref[...]


def ada_modulation(cond, w, b):
    n, d = cond.shape
    cols = w.shape[1]
    tn = 1024
    return pl.pallas_call(
        _ada_kernel,
        grid=(cols // tn,),
        in_specs=[pl.BlockSpec((n, d), lambda j: (0, 0)),
                  pl.BlockSpec((d, tn), lambda j: (0, j)),
                  pl.BlockSpec((1, tn), lambda j: (0, j))],
        out_specs=pl.BlockSpec((n, tn), lambda j: (0, j)),
        out_shape=jax.ShapeDtypeStruct((n, cols), F32),
        compiler_params=_cparams(("arbitrary",)),
        name="ada_modulation",
    )(cond, w, b.reshape(1, cols))


def _inproj_kernel(x_ref, nw_ref, sc_ref, sh_ref, w_ref, *rest, d_model, widths):
    o_refs = rest[:len(widths)]
    u_ref = rest[len(widths)]
    r = x_ref.shape[1]
    k = x_ref.shape[2] // d_model
    nw = nw_ref[...]
    sc = 1.0 + sc_ref[0]
    sh = sh_ref[0]
    for c in range(k):
        xs = x_ref[0, :, c * d_model:(c + 1) * d_model]
        u_ref[c * r:(c + 1) * r, :] = ((_rms(xs) * nw) * sc + sh).astype(BF16)
    off = 0
    for o_ref, wd in zip(o_refs, widths):
        step = min(wd, 512)
        for n0 in range(0, wd, step):
            o_ref[0, :, n0:n0 + step] = _dot(u_ref[...], w_ref[:, off + n0:off + n0 + step]).astype(o_ref.dtype)
        off += wd


def in_projection(x, norm_w, scale, shift, w, widths, dtypes, *, col_major, tokens_per_step):
    b_, l, d = x.shape
    if col_major:
        rows = l // GRID_W
        k = tokens_per_step // rows
        xv = x.reshape(b_, rows, GRID_W * d)
        x_spec = pl.BlockSpec((1, rows, k * d), lambda b, i: (b, 0, i))
        steps = GRID_W // k
    else:
        xv = x
        x_spec = pl.BlockSpec((1, tokens_per_step, d), lambda b, i: (b, i, 0))
        steps = l // tokens_per_step
    vec = lambda: pl.BlockSpec((1, 1, d), lambda b, i: (b, 0, 0))
    return pl.pallas_call(
        functools.partial(_inproj_kernel, d_model=d, widths=tuple(widths)),
        grid=(b_, steps),
        in_specs=[x_spec,
                  pl.BlockSpec((1, d), lambda b, i: (0, 0)),
                  vec(), vec(),
                  pl.BlockSpec(w.shape, lambda b, i: (0, 0), pipeline_mode=pl.Buffered(1))],
        out_specs=[pl.BlockSpec((1, tokens_per_step, wd), lambda b, i: (b, i, 0)) for wd in widths],
        out_shape=[jax.ShapeDtypeStruct((b_, l, wd), dt) for wd, dt in zip(widths, dtypes)],
        scratch_shapes=[pltpu.VMEM((tokens_per_step, d), BF16)],
        compiler_params=_cparams(("parallel", "arbitrary")),
        name="in_projection_col" if col_major else "in_projection_row",
    )(xv, norm_w.reshape(1, d), scale, shift, w)


def _s5_kernel(uc_ref, ul_ref, bexp_ref, cexp_ref, ar_ref, ai_ref, y_ref, *scratch, n_ctx, n_chunks, reverse):
    nslab = (len(scratch) - 3) // 2
    ubufs, ybufs = scratch[:nslab], scratch[nslab:2 * nslab]
    bu, xst, state = scratch[2 * nslab:]
    s = pl.program_id(1)
    nb, tl = ul_ref.shape[0], ul_ref.shape[1]
    half = state.shape[1] // 2
    p = s % 2
    q = 1 - p
    is_ctx = jnp.minimum(s, n_chunks - 1) < n_ctx

    @pl.when(s == 0)
    def _():
        bu[...] = jnp.zeros_like(bu)
        xst[...] = jnp.zeros_like(xst)
        state[...] = jnp.zeros_like(state)

    for b in range(nb):
        ub = jnp.where(is_ctx, uc_ref[b], ul_ref[b])
        for k, ubuf in enumerate(ubufs):
            ubuf[pl.ds(b, tl, stride=nb), :] = ub[:, k * LANES:(k + 1) * LANES]

    ar = ar_ref[0]
    ai = ai_ref[0]
    xr = state[:, 0:half]
    xi = state[:, half:2 * half]
    seg_tokens = tl // S5_SEGMENTS
    seg_rows = seg_tokens * nb
    for seg in range(S5_SEGMENTS):
        rows = pl.ds(seg * seg_rows, seg_rows)
        y_tm = _dot(xst[p, rows, :], cexp_ref[0])
        for k, ybuf in enumerate(ybufs):
            ybuf[rows, :] = y_tm[:, k * LANES:(k + 1) * LANES]
        u_tm = jnp.concatenate([ubuf[rows, :] for ubuf in ubufs], axis=1)
        bu[p, rows, :] = _dot(u_tm.astype(BF16), bexp_ref[0])
        sseg = S5_SEGMENTS - 1 - seg if reverse else seg
        pairs = range(sseg * seg_tokens // 2, (sseg + 1) * seg_tokens // 2)
        for m in (reversed(pairs) if reverse else pairs):
            new = {}
            for tok in ((2 * m + 1, 2 * m) if reverse else (2 * m, 2 * m + 1)):
                br = bu[q, tok * nb:(tok + 1) * nb, 0:half]
                bi = bu[q, tok * nb:(tok + 1) * nb, half:2 * half]
                xr, xi = ar * xr - ai * xi + br, ar * xi + ai * xr + bi
                new[tok] = (xr, xi)
            lo, hi = new[2 * m], new[2 * m + 1]
            xst[q, 2 * m * nb:(2 * m + 2) * nb, 0:half] = jnp.concatenate([lo[0], hi[0]], axis=0).astype(BF16)
            xst[q, 2 * m * nb:(2 * m + 2) * nb, half:2 * half] = jnp.concatenate([lo[1], hi[1]], axis=0).astype(BF16)
    state[:, 0:half] = xr
    state[:, half:2 * half] = xi

    for b in range(nb):
        for k, ybuf in enumerate(ybufs):
            y_ref[b, :, k * LANES:(k + 1) * LANES] = ybuf[pl.ds(b, tl, stride=nb), :].astype(y_ref.dtype)


def s5_scan(u_ctx, u_lat, bexp, cexp, ar, ai, *, reverse):
    b_, l_lat, w = u_lat.shape
    l_ctx = u_ctx.shape[1]
    tl = S5_CHUNK
    n_ctx, n_lat = l_ctx // tl, l_lat // tl
    n_chunks = n_ctx + n_lat
    ncb = w // S5_COLS
    ns = bexp.shape[-1]

    def ctx_idx(k):
        return jnp.maximum(n_ctx - 1 - k, 0) if reverse else jnp.minimum(k, n_ctx - 1)

    def lat_idx(k):
        kk = jnp.clip(k - n_ctx, 0, n_lat - 1)
        return n_lat - 1 - kk if reverse else kk

    in_pos = lambda s: jnp.minimum(s, n_chunks - 1)
    out_pos = lambda s: jnp.maximum(s - 2, 0)
    return pl.pallas_call(
        functools.partial(_s5_kernel, n_ctx=n_ctx, n_chunks=n_chunks, reverse=reverse),
        grid=(ncb, n_chunks + 2),
        in_specs=[pl.BlockSpec((b_, tl, S5_COLS), lambda j, s: (0, ctx_idx(in_pos(s)), j)),
                  pl.BlockSpec((b_, tl, S5_COLS), lambda j, s: (0, lat_idx(in_pos(s)), j)),
                  pl.BlockSpec((1, S5_COLS, ns), lambda j, s: (j, 0, 0)),
                  pl.BlockSpec((1, ns, S5_COLS), lambda j, s: (j, 0, 0)),
                  pl.BlockSpec((1, b_, ns // 2), lambda j, s: (j, 0, 0)),
                  pl.BlockSpec((1, b_, ns // 2), lambda j, s: (j, 0, 0))],
        out_specs=pl.BlockSpec((b_, tl, S5_COLS), lambda j, s: (0, lat_idx(out_pos(s)), j)),
        out_shape=jax.ShapeDtypeStruct((b_, l_lat, w), BF16),
        scratch_shapes=([pltpu.VMEM((tl * b_, LANES), F32)] * (2 * (S5_COLS // LANES))
                        + [pltpu.VMEM((2, tl * b_, ns), F32), pltpu.VMEM((2, tl * b_, ns), BF16),
                           pltpu.VMEM((b_, ns), F32)]),
        compiler_params=_cparams(("parallel", "arbitrary")),
        name="s5_scan_bwd" if reverse else "s5_scan_fwd",
    )(u_ctx, u_lat, bexp, cexp, ar, ai)


def _s5_expand(lam_re, lam_im, log_dt, b_re, b_im, c_re, c_im, batch):
    lam = lax.complex(lam_re.astype(F32), lam_im.astype(F32))
    step = jnp.exp(log_dt.astype(F32))[:, None]
    a_bar = jnp.exp(lam * step)
    b_mat = lax.complex(b_re.astype(F32), b_im.astype(F32))
    b_bar = ((a_bar - 1.0) / lam)[:, :, None] * b_mat
    g, p, h = b_bar.shape
    gl = S5_COLS // h
    ncb = g // gl
    eye = jnp.eye(gl, dtype=F32)

    def expand_b(t):
        return jnp.einsum('jgph,gk->jghkp', t.reshape(ncb, gl, p, h), eye).reshape(ncb, gl * h, gl * p)

    def expand_c(t):
        return jnp.einsum('jghp,gk->jgpkh', t.reshape(ncb, gl, h, p), eye).reshape(ncb, gl * p, gl * h)

    bexp = jnp.concatenate([expand_b(jnp.real(b_bar)), expand_b(jnp.imag(b_bar))], axis=-1)
    cexp = jnp.concatenate([expand_c(c_re.astype(F32)), -expand_c(c_im.astype(F32))], axis=1)
    ar = jnp.broadcast_to(jnp.real(a_bar).reshape(ncb, 1, gl * p), (ncb, batch, gl * p))
    ai = jnp.broadcast_to(jnp.imag(a_bar).reshape(ncb, 1, gl * p), (ncb, batch, gl * p))
    return bexp.astype(BF16), cexp.astype(BF16), ar, ai


def _s5_glu_kernel(yf_ref, yb_ref, u_ref, d_ref, gw_ref, gb_ref, wa_ref, o_ref):
    y = yf_ref[0].astype(F32) + yb_ref[0].astype(F32) + d_ref[...] * u_ref[0]
    g = _gelu_tanh(y)
    gate = _sigmoid(_dot(g.astype(BF16), gw_ref[...]) + gb_ref[...])
    o_ref[0] = _dot((g * gate).astype(BF16), wa_ref[...]).astype(o_ref.dtype)


def s5_glu_branch(yf, yb, u, s5_d, glu_w, glu_b, w_branch):
    b_, l, w = u.shape
    dm = w_branch.shape[1]
    tm = 256
    row = lambda: pl.BlockSpec((1, w), lambda b, i: (0, 0))
    return pl.pallas_call(
        _s5_glu_kernel,
        grid=(b_, l // tm),
        in_specs=[pl.BlockSpec((1, tm, w), lambda b, i: (b, i, 0)),
                  pl.BlockSpec((1, tm, w), lambda b, i: (b, i, 0)),
                  pl.BlockSpec((1, tm, w), lambda b, i: (b, i, 0)),
                  row(),
                  pl.BlockSpec((w, w), lambda b, i: (0, 0)),
                  row(),
                  pl.BlockSpec((w, dm), lambda b, i: (0, 0))],
        out_specs=pl.BlockSpec((1, tm, dm), lambda b, i: (b, i, 0)),
        out_shape=jax.ShapeDtypeStruct((b_, l, dm), BF16),
        compiler_params=_cparams(("parallel", "arbitrary")),
        name="s5_glu_branch",
    )(yf, yb, u, s5_d.reshape(1, w), glu_w, glu_b.reshape(1, w), w_branch)


def _ssd_prep_kernel(prev_ref, cur_ref, next_ref, dt_ref, cw_ref, cb_ref, dtb_ref, xo_ref, dto_ref, *, kw):
    i = pl.program_id(1)
    last = pl.num_programs(1) - 1
    tl = cur_ref.shape[1]
    halo = prev_ref.shape[1]
    pad = kw // 2
    prev = jnp.where(i == 0, 0.0, prev_ref[0])
    nxt = jnp.where(i == last, 0.0, next_ref[0])
    ext = jnp.concatenate([prev, cur_ref[0], nxt], axis=0)
    acc = cb_ref[...] + jnp.zeros((tl, cur_ref.shape[2]), F32)
    rows = ext.shape[0]
    for k in range(kw):
        shifted = ext if k == pad else pltpu.roll(ext, (pad - k) % rows, 0)
        acc = acc + shifted[halo:halo + tl, :] * cw_ref[k:k + 1, :]
    xo_ref[0] = _silu(acc)
    dto_ref[0] = _softplus(dt_ref[0] + dtb_ref[...])


def ssd_prep(xbc, dt_raw, conv_w, conv_b, dt_bias_row):
    b_, l, c = xbc.shape
    tl = min(256, l)
    halo = 8
    hb = tl // halo
    nh = l // halo
    kw = conv_w.shape[0]
    cw = jnp.zeros((8, c), F32).at[:kw].set(conv_w)
    dp = dt_raw.shape[2]
    return pl.pallas_call(
        functools.partial(_ssd_prep_kernel, kw=kw),
        grid=(b_, l // tl),
        in_specs=[pl.BlockSpec((1, halo, c), lambda b, i: (b, jnp.maximum(i * hb - 1, 0), 0)),
                  pl.BlockSpec((1, tl, c), lambda b, i: (b, i, 0)),
                  pl.BlockSpec((1, halo, c), lambda b, i: (b, jnp.minimum((i + 1) * hb, nh - 1), 0)),
                  pl.BlockSpec((1, tl, dp), lambda b, i: (b, i, 0)),
                  pl.BlockSpec((8, c), lambda b, i: (0, 0)),
                  pl.BlockSpec((1, c), lambda b, i: (0, 0)),
                  pl.BlockSpec((1, dp), lambda b, i: (0, 0))],
        out_specs=[pl.BlockSpec((1, tl, c), lambda b, i: (b, i, 0)),
                   pl.BlockSpec((1, tl, dp), lambda b, i: (b, i, 0))],
        out_shape=[jax.ShapeDtypeStruct((b_, l, c), F32),
                   jax.ShapeDtypeStruct((b_, l, dp), F32)],
        compiler_params=_cparams(("parallel", "arbitrary")),
        name="ssd_prep",
    )(xbc, xbc, xbc, dt_raw, cw, conv_b.reshape(1, c), dt_bias_row)


def _ssd_scan_kernel(*refs, reverse, width, col0, with_output):
    if with_output:
        x_ref, dt_ref, a_ref, e_ref, s0_ref, y_ref, state, bt_scr = refs
    else:
        x_ref, dt_ref, a_ref, e_ref, so_ref, state, bt_scr = refs
    i = pl.program_id(1)
    q = x_ref.shape[1]
    n = SSD_STATE
    p = SSD_HEADDIM
    ng = SSD_GROUPS
    gw = width // ng
    hpg = gw // p

    @pl.when(i == 0)
    def _():
        state[...] = s0_ref[0] if with_output else jnp.zeros_like(state)

    dt = dt_ref[0][:, :LANES]
    dta = dt * a_ref[...]
    ri = lax.broadcasted_iota(jnp.int32, (q, q), 0)
    ci = lax.broadcasted_iota(jnp.int32, (q, q), 1)
    mask = (ri <= ci) if reverse else (ri >= ci)
    tri = mask.astype(BF16)
    hi, mid, lo = _split3(dta)
    acum = _dot(tri, hi) + _dot(tri, mid) + _dot(tri, lo)

    def expand(v):
        h_, m_, l_ = _split3(v)
        return _dot(jnp.concatenate([h_, m_], axis=1), e_ref[...]) + _dot(l_, e_ref[0:LANES, :])

    acum_e = expand(acum)
    dt_e = expand(dt)
    last = 0 if reverse else q - 1
    tot_e = acum_e[last:last + 1, :]
    xdt = x_ref[0, :, 0:width] * dt_e
    xw = (xdt * jnp.exp(tot_e - acum_e)).astype(BF16)
    dec_e = jnp.exp(tot_e)
    for g in range(ng):
        bt_scr[g] = x_ref[0, :, width + g * n: width + (g + 1) * n].T.astype(BF16)

    if with_output:
        xdt_bf = xdt.astype(BF16)
        eac = jnp.exp(acum_e)
        acum_t = acum.T
        head_of_lane = lax.broadcasted_iota(jnp.int32, (1, gw), 1) // p
        for g in range(ng):
            sl = slice(g * gw, (g + 1) * gw)
            cmat = x_ref[0, :, width + (ng + g) * n: width + (ng + g + 1) * n].astype(BF16)
            cb = _dot(cmat, bt_scr[g])
            yoff = _dot(cmat, state[g].astype(BF16)) * eac[:, sl]
            xg = xdt_bf[:, sl]
            gms, rhs = [], []
            for r in range(hpg):
                hc = col0 + g * hpg + r
                diff = acum[:, hc:hc + 1] - acum_t[hc:hc + 1, :]
                gms.append((cb * jnp.exp(jnp.where(mask, diff, MASKED_EXPONENT))).astype(BF16))
                rhs.append(jnp.where(head_of_lane == r, xg, jnp.zeros_like(xg)))
            yd = _dot(jnp.concatenate(gms, axis=1), jnp.concatenate(rhs, axis=0))
            y_ref[0, :, sl] = (yd + yoff).astype(y_ref.dtype)

    for g in range(ng):
        sl = slice(g * gw, (g + 1) * gw)
        state[g] = state[g] * dec_e[:, sl] + _dot(bt_scr[g], xw[:, sl])
    if not with_output:
        so_ref[0] = state[...]


def ssd_scan(xc, dt, a_row, init_state, *, reverse, width):
    b_, l, c = xc.shape
    q = SSD_CHUNK
    nc = l // q
    dp = dt.shape[2]
    heads = width // SSD_HEADDIM
    col0 = heads if reverse else 0
    with_output = init_state is not None
    gw = width // SSD_GROUPS
    col = lax.broadcasted_iota(jnp.int32, (2 * LANES, width), 0) % LANES
    expand_mat = (col == col0 + lax.broadcasted_iota(jnp.int32, (2 * LANES, width), 1) // SSD_HEADDIM).astype(BF16)
    chunk = (lambda i: nc - 1 - i) if reverse else (lambda i: i)
    state_spec = pl.BlockSpec((1, SSD_GROUPS, SSD_STATE, gw), lambda b, i: (b, 0, 0, 0))
    in_specs = [pl.BlockSpec((1, q, c), lambda b, i: (b, chunk(i), 0)),
                pl.BlockSpec((1, q, dp), lambda b, i: (b, chunk(i), 0)),
                pl.BlockSpec((1, LANES), lambda b, i: (0, 0)),
                pl.BlockSpec((2 * LANES, width), lambda b, i: (0, 0))]
    args = [xc, dt, a_row, expand_mat]
    if with_output:
        in_specs.append(state_spec)
        args.append(init_state)
        out_specs = pl.BlockSpec((1, q, width), lambda b, i: (b, chunk(i), 0))
        out_shape = jax.ShapeDtypeStruct((b_, l, width), BF16)
    else:
        out_specs = state_spec
        out_shape = jax.ShapeDtypeStruct((b_, SSD_GROUPS, SSD_STATE, gw), F32)
    return pl.pallas_call(
        functools.partial(_ssd_scan_kernel, reverse=reverse, width=width, col0=col0, with_output=with_output),
        grid=(b_, nc),
        in_specs=in_specs,
        out_specs=out_specs,
        out_shape=out_shape,
        scratch_shapes=[pltpu.VMEM((SSD_GROUPS, SSD_STATE, gw), F32),
                        pltpu.VMEM((SSD_GROUPS, SSD_STATE, q), BF16)],
        compiler_params=_cparams(("parallel", "arbitrary")),
        name=("ssd_scan" if with_output else "ssd_state") + ("_bwd" if reverse else "_fwd"),
    )(*args)


def _ssd_out_kernel(yf_ref, yb_ref, x_ref, z_ref, d_ref, nw_ref, wb_ref, o_ref):
    y = ((yf_ref[0].astype(F32) + yb_ref[0].astype(F32) + d_ref[...] * x_ref[0])
         * _silu(z_ref[0].astype(F32)))
    w = y.shape[1]
    gw = w // SSD_GROUPS
    parts = [_rms(y[:, g * gw:(g + 1) * gw]) for g in range(SSD_GROUPS)]
    yn = (jnp.concatenate(parts, axis=1) * nw_ref[...]).astype(BF16)
    o_ref[0] = _dot(yn, wb_ref[...]).astype(o_ref.dtype)


def ssd_out_branch(yf, yb, xc_lat, z, d_row, norm_w, w_branch):
    b_, l, w = yf.shape
    dm = w_branch.shape[1]
    tm = 256
    blk = lambda: pl.BlockSpec((1, tm, w), lambda b, i: (b, i, 0))
    row = lambda: pl.BlockSpec((1, w), lambda b, i: (0, 0))
    return pl.pallas_call(
        _ssd_out_kernel,
        grid=(b_, l // tm),
        in_specs=[blk(), blk(), blk(), blk(), row(), row(),
                  pl.BlockSpec((w, dm), lambda b, i: (0, 0))],
        out_specs=pl.BlockSpec((1, tm, dm), lambda b, i: (b, i, 0)),
        out_shape=jax.ShapeDtypeStruct((b_, l, dm), BF16),
        compiler_params=_cparams(("parallel", "arbitrary")),
        name="ssd_out_branch",
    )(yf, yb, xc_lat, z, d_row, norm_w.reshape(1, w), w_branch)


def _merge_router_kernel(x_ref, ga_ref, gb_ref, ba_ref, bb_ref, gm_ref, wo_ref, nw_ref, sc_ref, sh_ref,
                         rw_ref, rb_ref, tri_ref, h_ref, xn_ref, ri_ref, rg_ref, cnt_ref, carry):
    @pl.when(jnp.logical_and(pl.program_id(0) == 0, pl.program_id(1) == 0))
    def _():
        carry[...] = jnp.zeros_like(carry)

    m = (_sigmoid(ga_ref[0].astype(F32)) * ba_ref[0].astype(F32)
         + _sigmoid(gb_ref[0].astype(F32)) * bb_ref[0].astype(F32))
    h = x_ref[0] + gm_ref[0] * _dot(m.astype(BF16), wo_ref[...])
    h_ref[0] = h
    u = (_rms(h) * nw_ref[...]) * (1.0 + sc_ref[0]) + sh_ref[0]
    xn_ref[0] = u.astype(BF16)
    uh, um, ul = _split3(u)
    wh, wm, wl = rw_ref[0], rw_ref[1], rw_ref[2]
    lg = (_dot(uh, wh) + (_dot(uh, wm) + _dot(um, wh))
          + (_dot(uh, wl) + _dot(um, wm) + _dot(ul, wh))) + rb_ref[...]

    tm, rp = lg.shape
    lane = lax.broadcasted_iota(jnp.int32, (tm, rp), 1)
    lane_f = lane.astype(F32)
    neg = jnp.float32(-jnp.inf)
    cur = jnp.where(lane < N_EXPERTS, lg, neg)
    onehots, vals, idxs = [], [], []
    for _ in range(TOP_K):
        mx = jnp.max(cur, axis=-1, keepdims=True)
        idx = jnp.min(jnp.where(cur == mx, lane_f, float(rp)), axis=-1, keepdims=True)
        oh = lane_f == idx
        onehots.append(oh)
        vals.append(mx)
        idxs.append(idx)
        cur = jnp.where(oh, neg, cur)
    exps = [jnp.exp(v - vals[0]) for v in vals]
    denom = exps[0] + exps[1] + exps[2] + exps[3]
    oh_all = (onehots[0] | onehots[1] | onehots[2] | onehots[3]).astype(F32)
    before = _dot(tri_ref[...], oh_all.astype(BF16)) + carry[...]
    ri = jnp.zeros((tm, rp), jnp.int32)
    rg = jnp.zeros((tm, rp), F32)
    for k in range(TOP_K):
        rank = jnp.sum(jnp.where(onehots[k], before, 0.0), axis=-1, keepdims=True).astype(jnp.int32)
        ri = jnp.where(lane == k, idxs[k].astype(jnp.int32), ri)
        ri = jnp.where(lane == TOP_K + k, rank, ri)
        rg = jnp.where(lane == k, exps[k] / denom, rg)
    ri_ref[0] = ri
    rg_ref[0] = rg
    carry[...] = carry[...] + jnp.sum(oh_all, axis=0, keepdims=True)
    cnt_ref[...] = jnp.broadcast_to(carry[...], cnt_ref.shape)


def merge_router(x, ga, gb, branch_a, branch_b, gate_m, w_out, norm_w, scale, shift, router_w3, router_b):
    b_, l, d = x.shape
    tm = 512
    rp = router_w3.shape[2]
    tri = (lax.broadcasted_iota(jnp.int32, (tm, tm), 0) > lax.broadcasted_iota(jnp.int32, (tm, tm), 1)).astype(BF16)
    blk = lambda w=d: pl.BlockSpec((1, tm, w), lambda b, i: (b, i, 0))
    vec = lambda: pl.BlockSpec((1, 1, d), lambda b, i: (b, 0, 0))
    return pl.pallas_call(
        _merge_router_kernel,
        grid=(b_, l // tm),
        in_specs=[blk(), blk(), blk(), blk(), blk(), vec(),
                  pl.BlockSpec((d, d), lambda b, i: (0, 0)),
                  pl.BlockSpec((1, d), lambda b, i: (0, 0)),
                  vec(), vec(),
                  pl.BlockSpec((3, d, rp), lambda b, i: (0, 0, 0)),
                  pl.BlockSpec((1, rp), lambda b, i: (0, 0)),
                  pl.BlockSpec((tm, tm), lambda b, i: (0, 0))],
        out_specs=[blk(), blk(), blk(rp), blk(rp),
                   pl.BlockSpec((8, rp), lambda b, i: (0, 0))],
        out_shape=[jax.ShapeDtypeStruct((b_, l, d), F32),
                   jax.ShapeDtypeStruct((b_, l, d), BF16),
                   jax.ShapeDtypeStruct((b_, l, rp), jnp.int32),
                   jax.ShapeDtypeStruct((b_, l, rp), F32),
                   jax.ShapeDtypeStruct((8, rp), F32)],
        scratch_shapes=[pltpu.VMEM((1, rp), F32)],
        compiler_params=_cparams(("arbitrary", "arbitrary")),
        name="merge_router",
    )(x, ga, gb, branch_a, branch_b, gate_m, w_out, norm_w.reshape(1, d), scale, shift,
      router_w3, router_b, tri)


def _expert_kernel(be_ref, nu_ref, x_ref, wg_ref, bg_ref, wu_ref, bu_ref, wd_ref, bd_ref, o_ref,
                   wg_bf, wu_bf, wd_bf):
    i = pl.program_id(0)
    new_expert = jnp.logical_or(i == 0, be_ref[i] != be_ref[jnp.maximum(i - 1, 0)])

    @pl.when(jnp.logical_and(new_expert, i < nu_ref[0]))
    def _():
        wg_bf[...] = wg_ref[0].astype(BF16)
        wu_bf[...] = wu_ref[0].astype(BF16)
        wd_bf[...] = wd_ref[0].astype(BF16)

    @pl.when(i < nu_ref[0])
    def _():
        xb = x_ref[...]
        gate = jnp.minimum(_dot(xb, wg_bf[...]) + bg_ref[0], SWIGLU_LIMIT)
        up = jnp.clip(_dot(xb, wu_bf[...]) + bu_ref[0], -SWIGLU_LIMIT, SWIGLU_LIMIT)
        act = (up + 1.0) * gate * _sigmoid(SWIGLU_ALPHA * gate)
        o_ref[...] = (_dot(act.astype(BF16), wd_bf[...]) + bd_ref[0]).astype(o_ref.dtype)

    @pl.when(i >= nu_ref[0])
    def _():
        o_ref[...] = jnp.zeros_like(o_ref)


def expert_blocks(block_expert, n_used, x_sorted, wg, bg, wu, bu, wd, bd):
    n_slots, d = x_sorted.shape
    de = wg.shape[2]
    nblk = n_slots // MOE_BLOCK
    wspec = lambda a, b: pl.BlockSpec((1, a, b), lambda i, be, nu: (be[i], 0, 0))
    grid_spec = pltpu.PrefetchScalarGridSpec(
        num_scalar_prefetch=2,
        grid=(nblk,),
        in_specs=[pl.BlockSpec((MOE_BLOCK, d), lambda i, be, nu: (i, 0)),
                  wspec(d, de), wspec(1, de), wspec(d, de), wspec(1, de), wspec(de, d), wspec(1, d)],
        out_specs=pl.BlockSpec((MOE_BLOCK, d), lambda i, be, nu: (i, 0)),
        scratch_shapes=[pltpu.VMEM((d, de), BF16), pltpu.VMEM((d, de), BF16), pltpu.VMEM((de, d), BF16)],
    )
    return pl.pallas_call(
        _expert_kernel,
        grid_spec=grid_spec,
        out_shape=jax.ShapeDtypeStruct((n_slots, d), BF16),
        compiler_params=_cparams(("arbitrary",)),
        name="expert_blocks",
    )(block_expert, n_used, x_sorted, wg, bg, wu, bu, wd, bd)


def _final_kernel(h_ref, y_ref, rg_ref, g_ref, w_ref, o_ref):
    rg = rg_ref[0]
    ffn = y_ref[0, 0].astype(F32) * rg[:, 0:1]
    for k in range(1, TOP_K):
        ffn = ffn + y_ref[k, 0].astype(F32) * rg[:, k:k + 1]
    h = h_ref[0] + g_ref[0] * ffn
    o_ref[0] = _rms(h) * w_ref[...]


def final_combine_norm(h, y4, rg, gate_f, final_w):
    b_, l, d = h.shape
    tm = 512
    blk = lambda w=d: pl.BlockSpec((1, tm, w), lambda b, i: (b, i, 0))
    return pl.pallas_call(
        _final_kernel,
        grid=(b_, l // tm),
        in_specs=[blk(),
                  pl.BlockSpec((TOP_K, 1, tm, d), lambda b, i: (0, b, i, 0)),
                  blk(rg.shape[2]),
                  pl.BlockSpec((1, 1, d), lambda b, i: (b, 0, 0)),
                  pl.BlockSpec((1, d), lambda b, i: (0, 0))],
        out_specs=blk(),
        out_shape=jax.ShapeDtypeStruct((b_, l, d), F32),
        compiler_params=_cparams(("parallel", "arbitrary")),
        name="final_combine_norm",
    )(h, y4, rg, gate_f, final_w.reshape(1, d))


def _slot_tables(counts, n_pairs):
    padded = (counts + MOE_BLOCK - 1) // MOE_BLOCK * MOE_BLOCK
    padded_end = jnp.cumsum(padded)
    n_blocks = n_pairs // MOE_BLOCK + N_EXPERTS
    block_start = jnp.arange(n_blocks, dtype=jnp.int32) * MOE_BLOCK
    block_expert = jnp.minimum(jnp.sum(block_start[:, None] >= padded_end[None, :], axis=1),
                               N_EXPERTS - 1).astype(jnp.int32)
    n_used = (padded_end[-1] // MOE_BLOCK).astype(jnp.int32).reshape(1)
    return (padded_end - padded).astype(jnp.int32), block_expert, n_used, n_blocks * MOE_BLOCK


def _slot_tokens(pair_slot, counts, slot_start, block_expert):
    order = jnp.argsort(pair_slot.reshape(-1)).astype(jnp.int32)
    cstart = jnp.cumsum(counts) - counts
    n_blocks = block_expert.shape[0]
    s = jnp.arange(n_blocks * MOE_BLOCK, dtype=jnp.int32).reshape(n_blocks, MOE_BLOCK)
    rank = s + (cstart - slot_start)[block_expert][:, None]
    valid = s < (slot_start + counts)[block_expert][:, None]
    tok = order[jnp.clip(rank, 0, order.shape[0] - 1).reshape(-1)] // TOP_K
    return jnp.where(valid.reshape(-1), tok, 0)


def kernel(x, c, ctx, c_ctx, ada_w, ada_b, norm_mix_w, w_in, s5_lam_re, s5_lam_im, s5_log_dt, s5_b_re, s5_b_im, s5_c_re, s5_c_im, s5_d, s5_glu_w, s5_glu_b, ssd_conv_w, ssd_conv_b, ssd_dt_bias, ssd_a_log, ssd_d, ssd_norm_w, w_branch_a, w_branch_b, w_out, norm_ffn_w, router_w, router_b, moe_w_gate, moe_b_gate, moe_w_up, moe_b_up, moe_w_down, moe_b_down, final_norm_w):
    assert ada_w.shape[0] == 1, "single-layer block"
    b_, l, d = x.shape
    s5_w = s5_d.shape[1]
    ssd_w = ssd_d.shape[1] * SSD_HEADDIM
    heads = ssd_d.shape[1]
    conv_dim = ssd_w + 2 * SSD_GROUPS * SSD_STATE
    c0, c1, c2, c3, c4 = (s5_w, s5_w + conv_dim, s5_w + conv_dim + 2 * heads,
                          s5_w + conv_dim + 2 * heads + ssd_w, s5_w + conv_dim + 2 * heads + ssd_w + d)

    cond = jnp.concatenate([c, c_ctx[None, :], jnp.zeros((7, d), F32)], axis=0)
    mod = ada_modulation(cond, ada_w[0], ada_b[0])
    mod6 = mod.reshape(mod.shape[0], 6, d)
    lat_mod = [mod6[:b_, k][:, None, :] for k in range(6)]
    ctx_mod = [jnp.broadcast_to(mod6[b_, k][None, None, :], (b_, 1, d)) for k in range(6)]
    sh_m, sc_m, g_m, sh_f, sc_f, g_f = lat_mod

    w = w_in[0]
    w_row = jnp.concatenate([w[:, :c0], w[:, c3:c4], w[:, c4:]], axis=1).astype(BF16)
    w_dt = jnp.zeros((d, DT_PAD), F32).at[:, :2 * heads].set(w[:, c1:c2])
    w_col = jnp.concatenate([w[:, c0:c1], w[:, c2:c3], w_dt], axis=1).astype(BF16)

    u_lat, ga, gb = in_projection(x, norm_mix_w[0], sc_m, sh_m, w_row, (s5_w, d, d), (F32, BF16, BF16),
                                  col_major=False, tokens_per_step=512)
    xbc_lat, z_lat, dtr_lat = in_projection(x, norm_mix_w[0], sc_m, sh_m, w_col, (conv_dim, ssd_w, DT_PAD),
                                            (F32, BF16, F32),
                                            col_major=True, tokens_per_step=256)
    lc = ctx.shape[1]
    (u_ctx,) = in_projection(ctx, norm_mix_w[0], ctx_mod[1], ctx_mod[0], w_row[:, :s5_w], (s5_w,), (F32,),
                             col_major=False, tokens_per_step=lc)
    w_col_ctx = jnp.concatenate([w_col[:, :conv_dim], w_col[:, conv_dim + ssd_w:]], axis=1)
    xbc_ctx, dtr_ctx = in_projection(ctx, norm_mix_w[0], ctx_mod[1], ctx_mod[0], w_col_ctx, (conv_dim, DT_PAD),
                                     (F32, F32),
                                     col_major=False, tokens_per_step=lc)

    exp_f = _s5_expand(s5_lam_re[0, 0], s5_lam_im[0, 0], s5_log_dt[0, 0], s5_b_re[0, 0], s5_b_im[0, 0],
                       s5_c_re[0, 0], s5_c_im[0, 0], b_)
    exp_b = _s5_expand(s5_lam_re[0, 1], s5_lam_im[0, 1], s5_log_dt[0, 1], s5_b_re[0, 1], s5_b_im[0, 1],
                       s5_c_re[0, 1], s5_c_im[0, 1], b_)
    ya_f = s5_scan(u_ctx, u_lat, *exp_f, reverse=False)
    ya_b = s5_scan(u_ctx, u_lat, *exp_b, reverse=True)
    branch_a = s5_glu_branch(ya_f, ya_b, u_lat, s5_d[0], s5_glu_w[0].astype(BF16), s5_glu_b[0],
                             w_branch_a[0].astype(BF16))

    dt_bias_row = jnp.zeros((1, DT_PAD), F32).at[0, :2 * heads].set(ssd_dt_bias[0].reshape(-1))
    a_row = jnp.zeros((1, LANES), F32).at[0, :2 * heads].set(-jnp.exp(ssd_a_log[0].astype(F32)).reshape(-1))
    xc_lat, dt_lat = ssd_prep(xbc_lat, dtr_lat, ssd_conv_w[0], ssd_conv_b[0], dt_bias_row)
    xc_ctx, dt_ctx = ssd_prep(xbc_ctx, dtr_ctx, ssd_conv_w[0], ssd_conv_b[0], dt_bias_row)
    s_f = ssd_scan(xc_ctx, dt_ctx, a_row, None, reverse=False, width=ssd_w)
    s_b = ssd_scan(xc_ctx, dt_ctx, a_row, None, reverse=True, width=ssd_w)
    y_f = ssd_scan(xc_lat, dt_lat, a_row, s_f, reverse=False, width=ssd_w)
    y_b = ssd_scan(xc_lat, dt_lat, a_row, s_b, reverse=True, width=ssd_w)
    d_row = jnp.repeat(ssd_d[0].astype(F32), SSD_HEADDIM).reshape(1, ssd_w)
    branch_b = ssd_out_branch(y_f, y_b, xc_lat, z_lat, d_row, ssd_norm_w[0], w_branch_b[0].astype(BF16))

    branch_b_rm = branch_b.reshape(b_, GRID_W, l // GRID_W, d).transpose(0, 2, 1, 3).reshape(b_, l, d)

    rw = jnp.zeros((d, ROUTER_PAD), F32).at[:, :N_EXPERTS].set(router_w[0])
    rw3 = jnp.stack(_split3(rw))
    rb = jnp.zeros((1, ROUTER_PAD), F32).at[0, :N_EXPERTS].set(router_b[0])
    h1, xn, ri, rg, cnt = merge_router(x, ga, gb, branch_a, branch_b_rm, g_m, w_out[0].astype(BF16),
                                       norm_ffn_w[0], sc_f, sh_f, rw3, rb)
    n = b_ * l
    counts = cnt[0, :N_EXPERTS].astype(jnp.int32)
    slot_start, block_expert, n_used, n_slots = _slot_tables(counts, n * TOP_K)
    ri = ri.reshape(n, ROUTER_PAD)
    pair_slot = slot_start[ri[:, :TOP_K]] + ri[:, TOP_K:2 * TOP_K]
    slot_token = _slot_tokens(pair_slot, counts, slot_start, block_expert)
    x_sorted = xn.reshape(n, d)[slot_token]
    y_sorted = expert_blocks(block_expert, n_used, x_sorted,
                             moe_w_gate[0], moe_b_gate[0][:, None, :],
                             moe_w_up[0], moe_b_up[0][:, None, :],
                             moe_w_down[0], moe_b_down[0][:, None, :])
    y4 = y_sorted[pair_slot.T.reshape(-1)].reshape(TOP_K, b_, l, d)

    return final_combine_norm(h1, y4, rg, g_f, final_norm_w)
```

```python
import functools
import math

import jax
import jax.numpy as jnp
from jax import lax
from jax.experimental import pallas as pl
from jax.experimental.pallas import tpu as pltpu

F32 = jnp.float32
BF16 = jnp.bfloat16

EPS = 1e-6
GRID_W = 64
LANES = 128

S5_GROUP = 16
S5_STATE = 64
S5_COLS = 256
S5_CHUNK = 128
S5_SEGMENTS = 1

SSD_HEADDIM = 64
SSD_GROUPS = 8
SSD_STATE = 128
SSD_CHUNK = 128
DT_PAD = 256
MASKED_EXPONENT = -1e30

N_EXPERTS = 32
TOP_K = 4
SWIGLU_ALPHA = 1.702
SWIGLU_LIMIT = 7.0
MOE_BLOCK = 512
ROUTER_PAD = 128

VMEM_LIMIT_BYTES = 56 * 1024 * 1024


def _cparams(sem):
    return pltpu.CompilerParams(dimension_semantics=sem, vmem_limit_bytes=VMEM_LIMIT_BYTES)


def _dot(a, b):
    return jnp.dot(a, b, preferred_element_type=F32)


def _split3(x):
    hi = x.astype(BF16)
    r = x - hi.astype(F32)
    mid = r.astype(BF16)
    lo = (r - mid.astype(F32)).astype(BF16)
    return hi, mid, lo


def _sigmoid(x):
    return 0.5 * (1.0 + jnp.tanh(0.5 * x))


def _silu(x):
    return x * _sigmoid(x)


def _softplus(x):
    return jnp.maximum(x, 0.0) + jnp.log1p(jnp.exp(-jnp.abs(x)))


def _gelu_tanh(x):
    c = math.sqrt(2.0 / math.pi)
    return 0.5 * x * (1.0 + jnp.tanh(c * (x + 0.044715 * (x * x * x))))


def _rms(x):
    return x * lax.rsqrt(jnp.mean(x * x, axis=-1, keepdims=True) + EPS)


def _ada_kernel(c_ref, w_ref, b_ref, o_ref):
    s = _silu(c_ref[...]).astype(BF16)
    o_ref[...] = _dot(s, w_ref[...].astype(BF16)) + b_ref[...]


def ada_modulation(cond, w, b):
    n, d = cond.shape
    cols = w.shape[1]
    tn = 1024
    return pl.pallas_call(
        _ada_kernel,
        grid=(cols // tn,),
        in_specs=[pl.BlockSpec((n, d), lambda j: (0, 0)),
                  pl.BlockSpec((d, tn), lambda j: (0, j)),
                  pl.BlockSpec((1, tn), lambda j: (0, j))],
        out_specs=pl.BlockSpec((n, tn), lambda j: (0, j)),
        out_shape=jax.ShapeDtypeStruct((n, cols), F32),
        compiler_params=_cparams(("arbitrary",)),
        name="ada_modulation",
    )(cond, w, b.reshape(1, cols))


def _inproj_kernel(x_ref, nw_ref, sc_ref, sh_ref, w_ref, *rest, d_model, widths):
    o_refs = rest[:len(widths)]
    u_ref = rest[len(widths)]
    r = x_ref.shape[1]
    k = x_ref.shape[2] // d_model
    nw = nw_ref[...]
    sc = 1.0 + sc_ref[0]
    sh = sh_ref[0]
    for c in range(k):
        xs = x_ref[0, :, c * d_model:(c + 1) * d_model]
        u_ref[c * r:(c + 1) * r, :] = ((_rms(xs) * nw) * sc + sh).astype(BF16)
    off = 0
    for o_ref, wd in zip(o_refs, widths):
        step = min(wd, 512)
        for n0 in range(0, wd, step):
            o_ref[0, :, n0:n0 + step] = _dot(u_ref[...], w_ref[:, off + n0:off + n0 + step]).astype(o_ref.dtype)
        off += wd


def in_projection(x, norm_w, scale, shift, w, widths, dtypes, *, col_major, tokens_per_step):
    b_, l, d = x.shape
    if col_major:
        rows = l // GRID_W
        k = tokens_per_step // rows
        xv = x.reshape(b_, rows, GRID_W * d)
        x_spec = pl.BlockSpec((1, rows, k * d), lambda b, i: (b, 0, i))
        steps = GRID_W // k
    else:
        xv = x
        x_spec = pl.BlockSpec((1, tokens_per_step, d), lambda b, i: (b, i, 0))
        steps = l // tokens_per_step
    vec = lambda: pl.BlockSpec((1, 1, d), lambda b, i: (b, 0, 0))
    return pl.pallas_call(
        functools.partial(_inproj_kernel, d_model=d, widths=tuple(widths)),
        grid=(b_, steps),
        in_specs=[x_spec,
                  pl.BlockSpec((1, d), lambda b, i: (0, 0)),
                  vec(), vec(),
                  pl.BlockSpec(w.shape, lambda b, i: (0, 0), pipeline_mode=pl.Buffered(1))],
        out_specs=[pl.BlockSpec((1, tokens_per_step, wd), lambda b, i: (b, i, 0)) for wd in widths],
        out_shape=[jax.ShapeDtypeStruct((b_, l, wd), dt) for wd, dt in zip(widths, dtypes)],
        scratch_shapes=[pltpu.VMEM((tokens_per_step, d), BF16)],
        compiler_params=_cparams(("parallel", "arbitrary")),
        name="in_projection_col" if col_major else "in_projection_row",
    )(xv, norm_w.reshape(1, d), scale, shift, w)


def _s5_kernel(uc_ref, ul_ref, bexp_ref, cexp_ref, ar_ref, ai_ref, y_ref, *scratch, n_ctx, n_chunks, reverse):
    nslab = (len(scratch) - 3) // 2
    ubufs, ybufs = scratch[:nslab], scratch[nslab:2 * nslab]
    bu, xst, state = scratch[2 * nslab:]
    s = pl.program_id(1)
    nb, tl = ul_ref.shape[0], ul_ref.shape[1]
    half = state.shape[1] // 2
    p = s % 2
    q = 1 - p
    is_ctx = jnp.minimum(s, n_chunks - 1) < n_ctx

    @pl.when(s == 0)
    def _():
        bu[...] = jnp.zeros_like(bu)
        xst[...] = jnp.zeros_like(xst)
        state[...] = jnp.zeros_like(state)

    for b in range(nb):
        ub = jnp.where(is_ctx, uc_ref[b], ul_ref[b])
        for k, ubuf in enumerate(ubufs):
            ubuf[pl.ds(b, tl, stride=nb), :] = ub[:, k * LANES:(k + 1) * LANES]

    ar = ar_ref[0]
    ai = ai_ref[0]
    xr = state[:, 0:half]
    xi = state[:, half:2 * half]
    seg_tokens = tl // S5_SEGMENTS
    seg_rows = seg_tokens * nb
    for seg in range(S5_SEGMENTS):
        rows = pl.ds(seg * seg_rows, seg_rows)
        y_tm = _dot(xst[p, rows, :], cexp_ref[0])
        for k, ybuf in enumerate(ybufs):
            ybuf[rows, :] = y_tm[:, k * LANES:(k + 1) * LANES]
        u_tm = jnp.concatenate([ubuf[rows, :] for ubuf in ubufs], axis=1)
        bu[p, rows, :] = _dot(u_tm.astype(BF16), bexp_ref[0])
        sseg = S5_SEGMENTS - 1 - seg if reverse else seg
        pairs = range(sseg * seg_tokens // 2, (sseg + 1) * seg_tokens // 2)
        for m in (reversed(pairs) if reverse else pairs):
            new = {}
            for tok in ((2 * m + 1, 2 * m) if reverse else (2 * m, 2 * m + 1)):
                br = bu[q, tok * nb:(tok + 1) * nb, 0:half]
                bi = bu[q, tok * nb:(tok + 1) * nb, half:2 * half]
                xr, xi = ar * xr - ai * xi + br, ar * xi + ai * xr + bi
                new[tok] = (xr, xi)
            lo, hi = new[2 * m], new[2 * m + 1]
            xst[q, 2 * m * nb:(2 * m + 2) * nb, 0:half] = jnp.concatenate([lo[0], hi[0]], axis=0).astype(BF16)
            xst[q, 2 * m * nb:(2 * m + 2) * nb, half:2 * half] = jnp.concatenate([lo[1], hi[1]], axis=0).astype(BF16)
    state[:, 0:half] = xr
    state[:, half:2 * half] = xi

    for b in range(nb):
        for k, ybuf in enumerate(ybufs):
            y_ref[b, :, k * LANES:(k + 1) * LANES] = ybuf[pl.ds(b, tl, stride=nb), :].astype(y_ref.dtype)


def s5_scan(u_ctx, u_lat, bexp, cexp, ar, ai, *, reverse):
    b_, l_lat, w = u_lat.shape
    l_ctx = u_ctx.shape[1]
    tl = S5_CHUNK
    n_ctx, n_lat = l_ctx // tl, l_lat // tl
    n_chunks = n_ctx + n_lat
    ncb = w // S5_COLS
    ns = bexp.shape[-1]

    def ctx_idx(k):
        return jnp.maximum(n_ctx - 1 - k, 0) if reverse else jnp.minimum(k, n_ctx - 1)

    def lat_idx(k):
        kk = jnp.clip(k - n_ctx, 0, n_lat - 1)
        return n_lat - 1 - kk if reverse else kk

    in_pos = lambda s: jnp.minimum(s, n_chunks - 1)
    out_pos = lambda s: jnp.maximum(s - 2, 0)
    return pl.pallas_call(
        functools.partial(_s5_kernel, n_ctx=n_ctx, n_chunks=n_chunks, reverse=reverse),
        grid=(ncb, n_chunks + 2),
        in_specs=[pl.BlockSpec((b_, tl, S5_COLS), lambda j, s: (0, ctx_idx(in_pos(s)), j)),
                  pl.BlockSpec((b_, tl, S5_COLS), lambda j, s: (0, lat_idx(in_pos(s)), j)),
                  pl.BlockSpec((1, S5_COLS, ns), lambda j, s: (j, 0, 0)),
                  pl.BlockSpec((1, ns, S5_COLS), lambda j, s: (j, 0, 0)),
                  pl.BlockSpec((1, b_, ns // 2), lambda j, s: (j, 0, 0)),
                  pl.BlockSpec((1, b_, ns // 2), lambda j, s: (j, 0, 0))],
        out_specs=pl.BlockSpec((b_, tl, S5_COLS), lambda j, s: (0, lat_idx(out_pos(s)), j)),
        out_shape=jax.ShapeDtypeStruct((b_, l_lat, w), BF16),
        scratch_shapes=([pltpu.VMEM((tl * b_, LANES), F32)] * (2 * (S5_COLS // LANES))
                        + [pltpu.VMEM((2, tl * b_, ns), F32), pltpu.VMEM((2, tl * b_, ns), BF16),
                           pltpu.VMEM((b_, ns), F32)]),
        compiler_params=_cparams(("parallel", "arbitrary")),
        name="s5_scan_bwd" if reverse else "s5_scan_fwd",
    )(u_ctx, u_lat, bexp, cexp, ar, ai)


def _s5_expand(lam_re, lam_im, log_dt, b_re, b_im, c_re, c_im, batch):
    lam = lax.complex(lam_re.astype(F32), lam_im.astype(F32))
    step = jnp.exp(log_dt.astype(F32))[:, None]
    a_bar = jnp.exp(lam * step)
    b_mat = lax.complex(b_re.astype(F32), b_im.astype(F32))
    b_bar = ((a_bar - 1.0) / lam)[:, :, None] * b_mat
    g, p, h = b_bar.shape
    gl = S5_COLS // h
    ncb = g // gl
    eye = jnp.eye(gl, dtype=F32)

    def expand_b(t):
        return jnp.einsum('jgph,gk->jghkp', t.reshape(ncb, gl, p, h), eye).reshape(ncb, gl * h, gl * p)

    def expand_c(t):
        return jnp.einsum('jghp,gk->jgpkh', t.reshape(ncb, gl, h, p), eye).reshape(ncb, gl * p, gl * h)

    bexp = jnp.concatenate([expand_b(jnp.real(b_bar)), expand_b(jnp.imag(b_bar))], axis=-1)
    cexp = jnp.concatenate([expand_c(c_re.astype(F32)), -expand_c(c_im.astype(F32))], axis=1)
    ar = jnp.broadcast_to(jnp.real(a_bar).reshape(ncb, 1, gl * p), (ncb, batch, gl * p))
    ai = jnp.broadcast_to(jnp.imag(a_bar).reshape(ncb, 1, gl * p), (ncb, batch, gl * p))
    return bexp.astype(BF16), cexp.astype(BF16), ar, ai


def _s5_glu_kernel(yf_ref, yb_ref, u_ref, d_ref, gw_ref, gb_ref, wa_ref, o_ref):
    y = yf_ref[0].astype(F32) + yb_ref[0].astype(F32) + d_ref[...] * u_ref[0]
    g = _gelu_tanh(y)
    gate = _sigmoid(_dot(g.astype(BF16), gw_ref[...]) + gb_ref[...])
    o_ref[0] = _dot((g * gate).astype(BF16), wa_ref[...]).astype(o_ref.dtype)


def s5_glu_branch(yf, yb, u, s5_d, glu_w, glu_b, w_branch):
    b_, l, w = u.shape
    dm = w_branch.shape[1]
    tm = 256
    row = lambda: pl.BlockSpec((1, w), lambda b, i: (0, 0))
    return pl.pallas_call(
        _s5_glu_kernel,
        grid=(b_, l // tm),
        in_specs=[pl.BlockSpec((1, tm, w), lambda b, i: (b, i, 0)),
                  pl.BlockSpec((1, tm, w), lambda b, i: (b, i, 0)),
                  pl.BlockSpec((1, tm, w), lambda b, i: (b, i, 0)),
                  row(),
                  pl.BlockSpec((w, w), lambda b, i: (0, 0)),
                  row(),
                  pl.BlockSpec((w, dm), lambda b, i: (0, 0))],
        out_specs=pl.BlockSpec((1, tm, dm), lambda b, i: (b, i, 0)),
        out_shape=jax.ShapeDtypeStruct((b_, l, dm), BF16),
        compiler_params=_cparams(("parallel", "arbitrary")),
        name="s5_glu_branch",
    )(yf, yb, u, s5_d.reshape(1, w), glu_w, glu_b.reshape(1, w), w_branch)


def _ssd_prep_kernel(prev_ref, cur_ref, next_ref, dt_ref, cw_ref, cb_ref, dtb_ref, xo_ref, dto_ref, *, kw):
    i = pl.program_id(1)
    last = pl.num_programs(1) - 1
    tl = cur_ref.shape[1]
    halo = prev_ref.shape[1]
    pad = kw // 2
    prev = jnp.where(i == 0, 0.0, prev_ref[0])
    nxt = jnp.where(i == last, 0.0, next_ref[0])
    ext = jnp.concatenate([prev, cur_ref[0], nxt], axis=0)
    acc = cb_ref[...] + jnp.zeros((tl, cur_ref.shape[2]), F32)
    rows = ext.shape[0]
    for k in range(kw):
        shifted = ext if k == pad else pltpu.roll(ext, (pad - k) % rows, 0)
        acc = acc + shifted[halo:halo + tl, :] * cw_ref[k:k + 1, :]
    xo_ref[0] = _silu(acc)
    dto_ref[0] = _softplus(dt_ref[0] + dtb_ref[...])


def ssd_prep(xbc, dt_raw, conv_w, conv_b, dt_bias_row):
    b_, l, c = xbc.shape
    tl = min(256, l)
    halo = 8
    hb = tl // halo
    nh = l // halo
    kw = conv_w.shape[0]
    cw = jnp.zeros((8, c), F32).at[:kw].set(conv_w)
    dp = dt_raw.shape[2]
    return pl.pallas_call(
        functools.partial(_ssd_prep_kernel, kw=kw),
        grid=(b_, l // tl),
        in_specs=[pl.BlockSpec((1, halo, c), lambda b, i: (b, jnp.maximum(i * hb - 1, 0), 0)),
                  pl.BlockSpec((1, tl, c), lambda b, i: (b, i, 0)),
                  pl.BlockSpec((1, halo, c), lambda b, i: (b, jnp.minimum((i + 1) * hb, nh - 1), 0)),
                  pl.BlockSpec((1, tl, dp), lambda b, i: (b, i, 0)),
                  pl.BlockSpec((8, c), lambda b, i: (0, 0)),
                  pl.BlockSpec((1, c), lambda b, i: (0, 0)),
                  pl.BlockSpec((1, dp), lambda b, i: (0, 0))],
        out_specs=[pl.BlockSpec((1, tl, c), lambda b, i: (b, i, 0)),
                   pl.BlockSpec((1, tl, dp), lambda b, i: (b, i, 0))],
        out_shape=[jax.ShapeDtypeStruct((b_, l, c), F32),
                   jax.ShapeDtypeStruct((b_, l, dp), F32)],
        compiler_params=_cparams(("parallel", "arbitrary")),
        name="ssd_prep",
    )(xbc, xbc, xbc, dt_raw, cw, conv_b.reshape(1, c), dt_bias_row)


def _ssd_scan_kernel(*refs, reverse, width, col0, with_output):
    if with_output:
        x_ref, dt_ref, a_ref, e_ref, s0_ref, y_ref, state, bt_scr = refs
    else:
        x_ref, dt_ref, a_ref, e_ref, so_ref, state, bt_scr = refs
    i = pl.program_id(1)
    q = x_ref.shape[1]
    n = SSD_STATE
    p = SSD_HEADDIM
    ng = SSD_GROUPS
    gw = width // ng
    hpg = gw // p

    @pl.when(i == 0)
    def _():
        state[...] = s0_ref[0] if with_output else jnp.zeros_like(state)

    dt = dt_ref[0][:, :LANES]
    dta = dt * a_ref[...]
    ri = lax.broadcasted_iota(jnp.int32, (q, q), 0)
    ci = lax.broadcasted_iota(jnp.int32, (q, q), 1)
    mask = (ri <= ci) if reverse else (ri >= ci)
    tri = mask.astype(BF16)
    hi, mid, lo = _split3(dta)
    acum = _dot(tri, hi) + _dot(tri, mid) + _dot(tri, lo)

    def expand(v):
        h_, m_, l_ = _split3(v)
        return _dot(jnp.concatenate([h_, m_], axis=1), e_ref[...]) + _dot(l_, e_ref[0:LANES, :])

    acum_e = expand(acum)
    dt_e = expand(dt)
    last = 0 if reverse else q - 1
    tot_e = acum_e[last:last + 1, :]
    xdt = x_ref[0, :, 0:width] * dt_e
    xw = (xdt * jnp.exp(tot_e - acum_e)).astype(BF16)
    dec_e = jnp.exp(tot_e)
    for g in range(ng):
        bt_scr[g] = x_ref[0, :, width + g * n: width + (g + 1) * n].T.astype(BF16)

    if with_output:
        xdt_bf = xdt.astype(BF16)
        eac = jnp.exp(acum_e)
        acum_t = acum.T
        head_of_lane = lax.broadcasted_iota(jnp.int32, (1, gw), 1) // p
        for g in range(ng):
            sl = slice(g * gw, (g + 1) * gw)
            cmat = x_ref[0, :, width + (ng + g) * n: width + (ng + g + 1) * n].astype(BF16)
            cb = _dot(cmat, bt_scr[g])
            yoff = _dot(cmat, state[g].astype(BF16)) * eac[:, sl]
            xg = xdt_bf[:, sl]
            gms, rhs = [], []
            for r in range(hpg):
                hc = col0 + g * hpg + r
                diff = acum[:, hc:hc + 1] - acum_t[hc:hc + 1, :]
                gms.append((cb * jnp.exp(jnp.where(mask, diff, MASKED_EXPONENT))).astype(BF16))
                rhs.append(jnp.where(head_of_lane == r, xg, jnp.zeros_like(xg)))
            yd = _dot(jnp.concatenate(gms, axis=1), jnp.concatenate(rhs, axis=0))
            y_ref[0, :, sl] = (yd + yoff).astype(y_ref.dtype)

    for g in range(ng):
        sl = slice(g * gw, (g + 1) * gw)
        state[g] = state[g] * dec_e[:, sl] + _dot(bt_scr[g], xw[:, sl])
    if not with_output:
        so_ref[0] = state[...]


def ssd_scan(xc, dt, a_row, init_state, *, reverse, width):
    b_, l, c = xc.shape
    q = SSD_CHUNK
    nc = l // q
    dp = dt.shape[2]
    heads = width // SSD_HEADDIM
    col0 = heads if reverse else 0
    with_output = init_state is not None
    gw = width // SSD_GROUPS
    col = lax.broadcasted_iota(jnp.int32, (2 * LANES, width), 0) % LANES
    expand_mat = (col == col0 + lax.broadcasted_iota(jnp.int32, (2 * LANES, width), 1) // SSD_HEADDIM).astype(BF16)
    chunk = (lambda i: nc - 1 - i) if reverse else (lambda i: i)
    state_spec = pl.BlockSpec((1, SSD_GROUPS, SSD_STATE, gw), lambda b, i: (b, 0, 0, 0))
    in_specs = [pl.BlockSpec((1, q, c), lambda b, i: (b, chunk(i), 0)),
                pl.BlockSpec((1, q, dp), lambda b, i: (b, chunk(i), 0)),
                pl.BlockSpec((1, LANES), lambda b, i: (0, 0)),
                pl.BlockSpec((2 * LANES, width), lambda b, i: (0, 0))]
    args = [xc, dt, a_row, expand_mat]
    if with_output:
        in_specs.append(state_spec)
        args.append(init_state)
        out_specs = pl.BlockSpec((1, q, width), lambda b, i: (b, chunk(i), 0))
        out_shape = jax.ShapeDtypeStruct((b_, l, width), BF16)
    else:
        out_specs = state_spec
        out_shape = jax.ShapeDtypeStruct((b_, SSD_GROUPS, SSD_STATE, gw), F32)
    return pl.pallas_call(
        functools.partial(_ssd_scan_kernel, reverse=reverse, width=width, col0=col0, with_output=with_output),
        grid=(b_, nc),
        in_specs=in_specs,
        out_specs=out_specs,
        out_shape=out_shape,
        scratch_shapes=[pltpu.VMEM((SSD_GROUPS, SSD_STATE, gw), F32),
                        pltpu.VMEM((SSD_GROUPS, SSD_STATE, q), BF16)],
        compiler_params=_cparams(("parallel", "arbitrary")),
        name=("ssd_scan" if with_output else "ssd_state") + ("_bwd" if reverse else "_fwd"),
    )(*args)


def _ssd_out_kernel(yf_ref, yb_ref, x_ref, z_ref, d_ref, nw_ref, wb_ref, o_ref):
    y = ((yf_ref[0].astype(F32) + yb_ref[0].astype(F32) + d_ref[...] * x_ref[0])
         * _silu(z_ref[0].astype(F32)))
    w = y.shape[1]
    gw = w // SSD_GROUPS
    parts = [_rms(y[:, g * gw:(g + 1) * gw]) for g in range(SSD_GROUPS)]
    yn = (jnp.concatenate(parts, axis=1) * nw_ref[...]).astype(BF16)
    o_ref[0] = _dot(yn, wb_ref[...]).astype(o_ref.dtype)


def ssd_out_branch(yf, yb, xc_lat, z, d_row, norm_w, w_branch):
    b_, l, w = yf.shape
    dm = w_branch.shape[1]
    tm = 256
    blk = lambda: pl.BlockSpec((1, tm, w), lambda b, i: (b, i, 0))
    row = lambda: pl.BlockSpec((1, w), lambda b, i: (0, 0))
    return pl.pallas_call(
        _ssd_out_kernel,
        grid=(b_, l // tm),
        in_specs=[blk(), blk(), blk(), blk(), row(), row(),
                  pl.BlockSpec((w, dm), lambda b, i: (0, 0))],
        out_specs=pl.BlockSpec((1, tm, dm), lambda b, i: (b, i, 0)),
        out_shape=jax.ShapeDtypeStruct((b_, l, dm), BF16),
        compiler_params=_cparams(("parallel", "arbitrary")),
        name="ssd_out_branch",
    )(yf, yb, xc_lat, z, d_row, norm_w.reshape(1, w), w_branch)


def _merge_router_kernel(x_ref, ga_ref, gb_ref, ba_ref, bb_ref, gm_ref, wo_ref, nw_ref, sc_ref, sh_ref,
                         rw_ref, rb_ref, tri_ref, h_ref, xn_ref, ri_ref, rg_ref, cnt_ref, carry):
    @pl.when(jnp.logical_and(pl.program_id(0) == 0, pl.program_id(1) == 0))
    def _():
        carry[...] = jnp.zeros_like(carry)

    m = (_sigmoid(ga_ref[0].astype(F32)) * ba_ref[0].astype(F32)
         + _sigmoid(gb_ref[0].astype(F32)) * bb_ref[0].astype(F32))
    h = x_ref[0] + gm_ref[0] * _dot(m.astype(BF16), wo_ref[...])
    h_ref[0] = h
    u = (_rms(h) * nw_ref[...]) * (1.0 + sc_ref[0]) + sh_ref[0]
    xn_ref[0] = u.astype(BF16)
    uh, um, ul = _split3(u)
    wh, wm, wl = rw_ref[0], rw_ref[1], rw_ref[2]
    lg = (_dot(uh, wh) + (_dot(uh, wm) + _dot(um, wh))
          + (_dot(uh, wl) + _dot(um, wm) + _dot(ul, wh))) + rb_ref[...]

    tm, rp = lg.shape
    lane = lax.broadcasted_iota(jnp.int32, (tm, rp), 1)
    lane_f = lane.astype(F32)
    neg = jnp.float32(-jnp.inf)
    cur = jnp.where(lane < N_EXPERTS, lg, neg)
    onehots, vals, idxs = [], [], []
    for _ in range(TOP_K):
        mx = jnp.max(cur, axis=-1, keepdims=True)
        idx = jnp.min(jnp.where(cur == mx, lane_f, float(rp)), axis=-1, keepdims=True)
        oh = lane_f == idx
        onehots.append(oh)
        vals.append(mx)
        idxs.append(idx)
        cur = jnp.where(oh, neg, cur)
    exps = [jnp.exp(v - vals[0]) for v in vals]
    denom = exps[0] + exps[1] + exps[2] + exps[3]
    oh_all = (onehots[0] | onehots[1] | onehots[2] | onehots[3]).astype(F32)
    before = _dot(tri_ref[...], oh_all.astype(BF16)) + carry[...]
    ri = jnp.zeros((tm, rp), jnp.int32)
    rg = jnp.zeros((tm, rp), F32)
    for k in range(TOP_K):
        rank = jnp.sum(jnp.where(onehots[k], before, 0.0), axis=-1, keepdims=True).astype(jnp.int32)
        ri = jnp.where(lane == k, idxs[k].astype(jnp.int32), ri)
        ri = jnp.where(lane == TOP_K + k, rank, ri)
        rg = jnp.where(lane == k, exps[k] / denom, rg)
    ri_ref[0] = ri
    rg_ref[0] = rg
    carry[...] = carry[...] + jnp.sum(oh_all, axis=0, keepdims=True)
    cnt_ref[...] = jnp.broadcast_to(carry[...], cnt_ref.shape)


def merge_router(x, ga, gb, branch_a, branch_b, gate_m, w_out, norm_w, scale, shift, router_w3, router_b):
    b_, l, d = x.shape
    tm = 512
    rp = router_w3.shape[2]
    tri = (lax.broadcasted_iota(jnp.int32, (tm, tm), 0) > lax.broadcasted_iota(jnp.int32, (tm, tm), 1)).astype(BF16)
    blk = lambda w=d: pl.BlockSpec((1, tm, w), lambda b, i: (b, i, 0))
    vec = lambda: pl.BlockSpec((1, 1, d), lambda b, i: (b, 0, 0))
    return pl.pallas_call(
        _merge_router_kernel,
        grid=(b_, l // tm),
        in_specs=[blk(), blk(), blk(), blk(), blk(), vec(),
                  pl.BlockSpec((d, d), lambda b, i: (0, 0)),
                  pl.BlockSpec((1, d), lambda b, i: (0, 0)),
                  vec(), vec(),
                  pl.BlockSpec((3, d, rp), lambda b, i: (0, 0, 0)),
                  pl.BlockSpec((1, rp), lambda b, i: (0, 0)),
                  pl.BlockSpec((tm, tm), lambda b, i: (0, 0))],
        out_specs=[blk(), blk(), blk(rp), blk(rp),
                   pl.BlockSpec((8, rp), lambda b, i: (0, 0))],
        out_shape=[jax.ShapeDtypeStruct((b_, l, d), F32),
                   jax.ShapeDtypeStruct((b_, l, d), BF16),
                   jax.ShapeDtypeStruct((b_, l, rp), jnp.int32),
                   jax.ShapeDtypeStruct((b_, l, rp), F32),
                   jax.ShapeDtypeStruct((8, rp), F32)],
        scratch_shapes=[pltpu.VMEM((1, rp), F32)],
        compiler_params=_cparams(("arbitrary", "arbitrary")),
        name="merge_router",
    )(x, ga, gb, branch_a, branch_b, gate_m, w_out, norm_w.reshape(1, d), scale, shift,
      router_w3, router_b, tri)


def _expert_kernel(be_ref, nu_ref, x_ref, wg_ref, bg_ref, wu_ref, bu_ref, wd_ref, bd_ref, o_ref,
                   wg_bf, wu_bf, wd_bf):
    i = pl.program_id(0)
    new_expert = jnp.logical_or(i == 0, be_ref[i] != be_ref[jnp.maximum(i - 1, 0)])

    @pl.when(jnp.logical_and(new_expert, i < nu_ref[0]))
    def _():
        wg_bf[...] = wg_ref[0].astype(BF16)
        wu_bf[...] = wu_ref[0].astype(BF16)
        wd_bf[...] = wd_ref[0].astype(BF16)

    @pl.when(i < nu_ref[0])
    def _():
        xb = x_ref[...]
        gate = jnp.minimum(_dot(xb, wg_bf[...]) + bg_ref[0], SWIGLU_LIMIT)
        up = jnp.clip(_dot(xb, wu_bf[...]) + bu_ref[0], -SWIGLU_LIMIT, SWIGLU_LIMIT)
        act = (up + 1.0) * gate * _sigmoid(SWIGLU_ALPHA * gate)
        o_ref[...] = (_dot(act.astype(BF16), wd_bf[...]) + bd_ref[0]).astype(o_ref.dtype)

    @pl.when(i >= nu_ref[0])
    def _():
        o_ref[...] = jnp.zeros_like(o_ref)


def expert_blocks(block_expert, n_used, x_sorted, wg, bg, wu, bu, wd, bd):
    n_slots, d = x_sorted.shape
    de = wg.shape[2]
    nblk = n_slots // MOE_BLOCK
    wspec = lambda a, b: pl.BlockSpec((1, a, b), lambda i, be, nu: (be[i], 0, 0))
    grid_spec = pltpu.PrefetchScalarGridSpec(
        num_scalar_prefetch=2,
        grid=(nblk,),
        in_specs=[pl.BlockSpec((MOE_BLOCK, d), lambda i, be, nu: (i, 0)),
                  wspec(d, de), wspec(1, de), wspec(d, de), wspec(1, de), wspec(de, d), wspec(1, d)],
        out_specs=pl.BlockSpec((MOE_BLOCK, d), lambda i, be, nu: (i, 0)),
        scratch_shapes=[pltpu.VMEM((d, de), BF16), pltpu.VMEM((d, de), BF16), pltpu.VMEM((de, d), BF16)],
    )
    return pl.pallas_call(
        _expert_kernel,
        grid_spec=grid_spec,
        out_shape=jax.ShapeDtypeStruct((n_slots, d), F32),
        compiler_params=_cparams(("arbitrary",)),
        name="expert_blocks",
    )(block_expert, n_used, x_sorted, wg, bg, wu, bu, wd, bd)


def _final_kernel(h_ref, y_ref, rg_ref, g_ref, w_ref, o_ref):
    rg = rg_ref[0]
    ffn = y_ref[0, 0].astype(F32) * rg[:, 0:1]
    for k in range(1, TOP_K):
        ffn = ffn + y_ref[k, 0].astype(F32) * rg[:, k:k + 1]
    h = h_ref[0] + g_ref[0] * ffn
    o_ref[0] = _rms(h) * w_ref[...]


def final_combine_norm(h, y4, rg, gate_f, final_w):
    b_, l, d = h.shape
    tm = 512
    blk = lambda w=d: pl.BlockSpec((1, tm, w), lambda b, i: (b, i, 0))
    return pl.pallas_call(
        _final_kernel,
        grid=(b_, l // tm),
        in_specs=[blk(),
                  pl.BlockSpec((TOP_K, 1, tm, d), lambda b, i: (0, b, i, 0)),
                  blk(rg.shape[2]),
                  pl.BlockSpec((1, 1, d), lambda b, i: (b, 0, 0)),
                  pl.BlockSpec((1, d), lambda b, i: (0, 0))],
        out_specs=blk(),
        out_shape=jax.ShapeDtypeStruct((b_, l, d), F32),
        compiler_params=_cparams(("parallel", "arbitrary")),
        name="final_combine_norm",
    )(h, y4, rg, gate_f, final_w.reshape(1, d))


def _slot_tables(counts, n_pairs):
    padded = (counts + MOE_BLOCK - 1) // MOE_BLOCK * MOE_BLOCK
    padded_end = jnp.cumsum(padded)
    n_blocks = n_pairs // MOE_BLOCK + N_EXPERTS
    block_start = jnp.arange(n_blocks, dtype=jnp.int32) * MOE_BLOCK
    block_expert = jnp.minimum(jnp.sum(block_start[:, None] >= padded_end[None, :], axis=1),
                               N_EXPERTS - 1).astype(jnp.int32)
    n_used = (padded_end[-1] // MOE_BLOCK).astype(jnp.int32).reshape(1)
    return (padded_end - padded).astype(jnp.int32), block_expert, n_used, n_blocks * MOE_BLOCK


def _slot_tokens(pair_slot, counts, slot_start, block_expert):
    order = jnp.argsort(pair_slot.reshape(-1)).astype(jnp.int32)
    cstart = jnp.cumsum(counts) - counts
    n_blocks = block_expert.shape[0]
    s = jnp.arange(n_blocks * MOE_BLOCK, dtype=jnp.int32).reshape(n_blocks, MOE_BLOCK)
    rank = s + (cstart - slot_start)[block_expert][:, None]
    valid = s < (slot_start + counts)[block_expert][:, None]
    tok = order[jnp.clip(rank, 0, order.shape[0] - 1).reshape(-1)] // TOP_K
    return jnp.where(valid.reshape(-1), tok, 0)


def kernel(x, c, ctx, c_ctx, ada_w, ada_b, norm_mix_w, w_in, s5_lam_re, s5_lam_im, s5_log_dt, s5_b_re, s5_b_im, s5_c_re, s5_c_im, s5_d, s5_glu_w, s5_glu_b, ssd_conv_w, ssd_conv_b, ssd_dt_bias, ssd_a_log, ssd_d, ssd_norm_w, w_branch_a, w_branch_b, w_out, norm_ffn_w, router_w, router_b, moe_w_gate, moe_b_gate, moe_w_up, moe_b_up, moe_w_down, moe_b_down, final_norm_w):
    assert ada_w.shape[0] == 1, "single-layer block"
    b_, l, d = x.shape
    s5_w = s5_d.shape[1]
    ssd_w = ssd_d.shape[1] * SSD_HEADDIM
    heads = ssd_d.shape[1]
    conv_dim = ssd_w + 2 * SSD_GROUPS * SSD_STATE
    c0, c1, c2, c3, c4 = (s5_w, s5_w + conv_dim, s5_w + conv_dim + 2 * heads,
                          s5_w + conv_dim + 2 * heads + ssd_w, s5_w + conv_dim + 2 * heads + ssd_w + d)

    cond = jnp.concatenate([c, c_ctx[None, :], jnp.zeros((7, d), F32)], axis=0)
    mod = ada_modulation(cond, ada_w[0], ada_b[0])
    mod6 = mod.reshape(mod.shape[0], 6, d)
    lat_mod = [mod6[:b_, k][:, None, :] for k in range(6)]
    ctx_mod = [jnp.broadcast_to(mod6[b_, k][None, None, :], (b_, 1, d)) for k in range(6)]
    sh_m, sc_m, g_m, sh_f, sc_f, g_f = lat_mod

    w = w_in[0]
    w_row = jnp.concatenate([w[:, :c0], w[:, c3:c4], w[:, c4:]], axis=1).astype(BF16)
    w_dt = jnp.zeros((d, DT_PAD), F32).at[:, :2 * heads].set(w[:, c1:c2])
    w_col = jnp.concatenate([w[:, c0:c1], w[:, c2:c3], w_dt], axis=1).astype(BF16)

    u_lat, ga, gb = in_projection(x, norm_mix_w[0], sc_m, sh_m, w_row, (s5_w, d, d), (F32, BF16, BF16),
                                  col_major=False, tokens_per_step=512)
    xbc_lat, z_lat, dtr_lat = in_projection(x, norm_mix_w[0], sc_m, sh_m, w_col, (conv_dim, ssd_w, DT_PAD),
                                            (F32, BF16, F32),
                                            col_major=True, tokens_per_step=256)
    lc = ctx.shape[1]
    (u_ctx,) = in_projection(ctx, norm_mix_w[0], ctx_mod[1], ctx_mod[0], w_row[:, :s5_w], (s5_w,), (F32,),
                             col_major=False, tokens_per_step=lc)
    w_col_ctx = jnp.concatenate([w_col[:, :conv_dim], w_col[:, conv_dim + ssd_w:]], axis=1)
    xbc_ctx, dtr_ctx = in_projection(ctx, norm_mix_w[0], ctx_mod[1], ctx_mod[0], w_col_ctx, (conv_dim, DT_PAD),
                                     (F32, F32),
                                     col_major=False, tokens_per_step=lc)

    exp_f = _s5_expand(s5_lam_re[0, 0], s5_lam_im[0, 0], s5_log_dt[0, 0], s5_b_re[0, 0], s5_b_im[0, 0],
                       s5_c_re[0, 0], s5_c_im[0, 0], b_)
    exp_b = _s5_expand(s5_lam_re[0, 1], s5_lam_im[0, 1], s5_log_dt[0, 1], s5_b_re[0, 1], s5_b_im[0, 1],
                       s5_c_re[0, 1], s5_c_im[0, 1], b_)
    ya_f = s5_scan(u_ctx, u_lat, *exp_f, reverse=False)
    ya_b = s5_scan(u_ctx, u_lat, *exp_b, reverse=True)
    branch_a = s5_glu_branch(ya_f, ya_b, u_lat, s5_d[0], s5_glu_w[0].astype(BF16), s5_glu_b[0],
                             w_branch_a[0].astype(BF16))

    dt_bias_row = jnp.zeros((1, DT_PAD), F32).at[0, :2 * heads].set(ssd_dt_bias[0].reshape(-1))
    a_row = jnp.zeros((1, LANES), F32).at[0, :2 * heads].set(-jnp.exp(ssd_a_log[0].astype(F32)).reshape(-1))
    xc_lat, dt_lat = ssd_prep(xbc_lat, dtr_lat, ssd_conv_w[0], ssd_conv_b[0], dt_bias_row)
    xc_ctx, dt_ctx = ssd_prep(xbc_ctx, dtr_ctx, ssd_conv_w[0], ssd_conv_b[0], dt_bias_row)
    s_f = ssd_scan(xc_ctx, dt_ctx, a_row, None, reverse=False, width=ssd_w)
    s_b = ssd_scan(xc_ctx, dt_ctx, a_row, None, reverse=True, width=ssd_w)
    y_f = ssd_scan(xc_lat, dt_lat, a_row, s_f, reverse=False, width=ssd_w)
    y_b = ssd_scan(xc_lat, dt_lat, a_row, s_b, reverse=True, width=ssd_w)
    d_row = jnp.repeat(ssd_d[0].astype(F32), SSD_HEADDIM).reshape(1, ssd_w)
    branch_b = ssd_out_branch(y_f, y_b, xc_lat, z_lat, d_row, ssd_norm_w[0], w_branch_b[0].astype(BF16))

    branch_b_rm = branch_b.reshape(b_, GRID_W, l // GRID_W, d).transpose(0, 2, 1, 3).reshape(b_, l, d)

    rw = jnp.zeros((d, ROUTER_PAD), F32).at[:, :N_EXPERTS].set(router_w[0])
    rw3 = jnp.stack(_split3(rw))
    rb = jnp.zeros((1, ROUTER_PAD), F32).at[0, :N_EXPERTS].set(router_b[0])
    h1, xn, ri, rg, cnt = merge_router(x, ga, gb, branch_a, branch_b_rm, g_m, w_out[0].astype(BF16),
                                       norm_ffn_w[0], sc_f, sh_f, rw3, rb)
    n = b_ * l
    counts = cnt[0, :N_EXPERTS].astype(jnp.int32)
    slot_start, block_expert, n_used, n_slots = _slot_tables(counts, n * TOP_K)
    ri = ri.reshape(n, ROUTER_PAD)
    pair_slot = slot_start[ri[:, :TOP_K]] + ri[:, TOP_K:2 * TOP_K]
    slot_token = _slot_tokens(pair_slot, counts, slot_start, block_expert)
    x_sorted = xn.reshape(n, d)[slot_token]
    y_sorted = expert_blocks(block_expert, n_used, x_sorted,
                             moe_w_gate[0], moe_b_gate[0][:, None, :],
                             moe_w_up[0], moe_b_up[0][:, None, :],
                             moe_w_down[0], moe_b_down[0][:, None, :])
    y4 = y_sorted[pair_slot.T.reshape(-1)].reshape(TOP_K, b_, l, d)

    return final_combine_norm(h1, y4, rg, g_f, final_norm_w)
```
